```python
import math
import jax
import jax.numpy as jnp
from jax import lax

D_MODEL = 2048
BATCH = 4
SEQ = 8192
DEPTH = 2

GRID_W = 64
CTX_LEN = 256
ROPE_BASE = 10000.0
NORM_EPS = 1e-6
NEG_INF = -1e30
Q_BLOCK = 128

POOL_GROUPS = 4
POOL_WINDOWS = (2, 4, 8, 16)
POOL_GROUP_DIM = D_MODEL // 8
POOL_WIDTH = POOL_GROUPS * POOL_GROUP_DIM
MLA_HEADS = 8
MLA_NOPE = 128
MLA_ROPE = 64
MLA_V = 128
MLA_KV_RANK = D_MODEL // 4
MLA_SCALE = (MLA_NOPE + MLA_ROPE) ** -0.5
WIN_HEADS = 8
WIN_KV_HEADS = 2
WIN_HEAD_DIM = 128
WINDOW = 128
WIN_SCALE = WIN_HEAD_DIM ** -0.5
DIFF_HEADS = 4
DIFF_HEAD_DIM = 128
DIFF_SCALE = DIFF_HEAD_DIM ** -0.5

N_BRANCHES = 4
D_FF = D_MODEL * 7 // 2
N_EXPERTS = 8
TOP_K = 2
EXPERT_FF = D_MODEL * 7 // 2

IN_SPLITS = (POOL_WIDTH,
             MLA_HEADS * (MLA_NOPE + MLA_ROPE), MLA_KV_RANK, MLA_ROPE,
             WIN_HEADS * WIN_HEAD_DIM, WIN_KV_HEADS * WIN_HEAD_DIM, WIN_KV_HEADS * WIN_HEAD_DIM,
             DIFF_HEADS * 2 * DIFF_HEAD_DIM, DIFF_HEADS * 2 * DIFF_HEAD_DIM, DIFF_HEADS * 2 * DIFF_HEAD_DIM,
             N_BRANCHES * D_MODEL)
IN_COLS = sum(IN_SPLITS)

kernel_name = 'hybrid_pool_mla_swa_diff_moe_dit'


def _rmsnorm(x, g):
    xf = x.astype(jnp.float32)
    y = xf * lax.rsqrt(jnp.mean(xf * xf, axis=-1, keepdims=True) + NORM_EPS)
    return (y * g.astype(jnp.float32)).astype(x.dtype)


def _modulate(x, g, shift, scale):
    return _rmsnorm(x, g) * (1.0 + scale) + shift


def _rotate_1d(x, pos, dim):
    half = dim // 2
    inv_freq = ROPE_BASE ** (-jnp.arange(half, dtype=jnp.float32) / half)
    ang = pos.astype(jnp.float32)[:, None] * inv_freq[None, :]
    shape = (pos.shape[0],) + (1,) * (x.ndim - 3) + (half,)
    cos = jnp.cos(ang).reshape(shape)
    sin = jnp.sin(ang).reshape(shape)
    xf = x.astype(jnp.float32)
    x1, x2 = xf[..., :half], xf[..., half:]
    return jnp.concatenate([x1 * cos - x2 * sin, x2 * cos + x1 * sin], axis=-1).astype(x.dtype)


def _axial_rope(x, rows, cols):
    h = x.shape[-1] // 2
    return jnp.concatenate([_rotate_1d(x[..., :h], rows, h), _rotate_1d(x[..., h:], cols, h)], axis=-1)


def _split_in(z):
    idx, acc = [], 0
    for s in IN_SPLITS[:-1]:
        acc += s
        idx.append(acc)
    return jnp.split(z, idx, axis=-1)


def _centred_mean_minus_self(x, w):
    n = x.shape[1]
    xf = x.astype(jnp.float32)
    csum = jnp.pad(lax.cumsum(xf, axis=1), ((0, 0), (1, 0), (0, 0)))
    t = jnp.arange(n)
    lo = jnp.clip(t - w // 2, 0, n)
    hi = jnp.clip(t - w // 2 + w, 0, n)
    s = jnp.take(csum, hi, axis=1) - jnp.take(csum, lo, axis=1)
    cnt = (hi - lo).astype(jnp.float32)
    return (s / cnt[None, :, None] - xf).astype(x.dtype)


def _pool_mixer(a, lp):
    bsz, n, _ = a.shape
    ag = a.reshape(bsz, n, POOL_GROUPS, POOL_GROUP_DIM)
    pooled = jnp.stack([_centred_mean_minus_self(ag[:, :, i], w) for i, w in enumerate(POOL_WINDOWS)], axis=2)
    mixed = jnp.einsum('bngc,gcd->bngd', pooled, lp['pool_w']).reshape(bsz, n, POOL_WIDTH)
    return (mixed * lp['pool_scale']) @ lp['pool_out']


def _project(u, lp, pos):
    bsz, n, _ = u.shape
    a_in, q_b, ckv, kr, q_c, k_c, v_c, q_d, k_d, v_d, gate_logits = _split_in(u @ lp['w_in'])
    if pos is None:
        rope = lambda t: t
    else:
        rope = lambda t: _axial_rope(t, pos[0], pos[1])
    q_b = q_b.reshape(bsz, n, MLA_HEADS, MLA_NOPE + MLA_ROPE)
    kv = (_rmsnorm(ckv, lp['mla_kv_norm_g']) @ lp['mla_w_kv_b']).reshape(bsz, n, MLA_HEADS, MLA_NOPE + MLA_V)
    k_rope = jnp.broadcast_to(rope(kr[:, :, None, :]), (bsz, n, MLA_HEADS, MLA_ROPE))
    return {
        'pool': a_in,
        'mla_q': jnp.concatenate([q_b[..., :MLA_NOPE], rope(q_b[..., MLA_NOPE:])], axis=-1),
        'mla_k': jnp.concatenate([kv[..., :MLA_NOPE], k_rope], axis=-1),
        'mla_v': kv[..., MLA_NOPE:],
        'win_q': rope(q_c.reshape(bsz, n, WIN_HEADS, WIN_HEAD_DIM)),
        'win_k': rope(k_c.reshape(bsz, n, WIN_KV_HEADS, WIN_HEAD_DIM)),
        'win_v': v_c.reshape(bsz, n, WIN_KV_HEADS, WIN_HEAD_DIM),
        'diff_q': rope(q_d.reshape(bsz, n, DIFF_HEADS, 2, DIFF_HEAD_DIM)),
        'diff_k': rope(k_d.reshape(bsz, n, DIFF_HEADS, 2, DIFF_HEAD_DIM)),
        'diff_v': v_d.reshape(bsz, n, DIFF_HEADS, 2 * DIFF_HEAD_DIM),
        'gates': jax.nn.sigmoid(gate_logits.reshape(bsz, n, N_BRANCHES, D_MODEL)),
    }


def _dense_attn(q, k, v, scale):
    b, sq, h, dk = q.shape
    nb = sq // Q_BLOCK
    qb = jnp.moveaxis(q.reshape(b, nb, Q_BLOCK, h, dk), 1, 0)

    def block(qi):
        s = jnp.einsum('bqhd,bkhd->bhqk', qi, k).astype(jnp.float32) * scale
        p = jax.nn.softmax(s, axis=-1).astype(v.dtype)
        return jnp.einsum('bhqk,bkhd->bqhd', p, v)

    o = lax.map(block, qb)
    return jnp.moveaxis(o, 0, 1).reshape(b, sq, h, v.shape[-1])


def _diff_attn(q, k, v, lam, scale):
    b, sq, h, _, dk = q.shape
    nb = sq // Q_BLOCK
    qb = jnp.moveaxis(q.reshape(b, nb, Q_BLOCK, h, 2, dk), 1, 0)

    def block(qi):
        s = jnp.einsum('bqhmd,bkhmd->bhmqk', qi, k).astype(jnp.float32) * scale
        p = jax.nn.softmax(s, axis=-1)
        p = (p[:, :, 0] - lam * p[:, :, 1]).astype(v.dtype)
        return jnp.einsum('bhqk,bkhd->bqhd', p, v)

    o = lax.map(block, qb)
    return jnp.moveaxis(o, 0, 1).reshape(b, sq, h, v.shape[-1])


def _window_attn(q, k, v, kc, vc, sink, scale):
    b, n, hq, d = q.shape
    hkv = k.shape[2]
    g = hq // hkv
    nb = n // Q_BLOCK
    span = Q_BLOCK + 2 * WINDOW
    ctx_len = kc.shape[1]
    padw = ((0, 0), (WINDOW, WINDOW), (0, 0), (0, 0))
    kp = jnp.pad(k, padw)
    vp = jnp.pad(v, padw)
    qb = jnp.moveaxis(q.reshape(b, nb, Q_BLOCK, hkv, g, d), 1, 0)
    sink_l = jnp.broadcast_to(sink.astype(jnp.float32).reshape(1, hkv, g, 1, 1), (b, hkv, g, Q_BLOCK, 1))
    rel = jnp.arange(span)[None, :] - WINDOW - jnp.arange(Q_BLOCK)[:, None]
    in_band = jnp.abs(rel) <= WINDOW

    def block(args):
        i, qi = args
        start = i * Q_BLOCK
        kw = lax.dynamic_slice_in_dim(kp, start, span, axis=1)
        vw = lax.dynamic_slice_in_dim(vp, start, span, axis=1)
        kpos = start - WINDOW + jnp.arange(span)
        valid = in_band & ((kpos >= 0) & (kpos < n))[None, :]
        s_loc = jnp.einsum('bqhgd,bkhd->bhgqk', qi, kw).astype(jnp.float32) * scale
        s_loc = jnp.where(valid, s_loc, NEG_INF)
        s_ctx = jnp.einsum('bqhgd,bkhd->bhgqk', qi, kc).astype(jnp.float32) * scale
        p = jax.nn.softmax(jnp.concatenate([s_loc, s_ctx, sink_l], axis=-1), axis=-1).astype(v.dtype)
        return (jnp.einsum('bhgqk,bkhd->bqhgd', p[..., :span], vw)
                + jnp.einsum('bhgqk,bkhd->bqhgd', p[..., span:span + ctx_len], vc))

    o = lax.map(block, (jnp.arange(nb), qb))
    return jnp.moveaxis(o, 0, 1).reshape(b, n, hq, d)


def _sink_attn(q, k, v, sink, scale):
    b, l, hq, d = q.shape
    hkv = k.shape[2]
    g = hq // hkv
    qg = q.reshape(b, l, hkv, g, d)
    s = jnp.einsum('bqhgd,bkhd->bhgqk', qg, k).astype(jnp.float32) * scale
    sk = jnp.broadcast_to(sink.astype(jnp.float32).reshape(1, hkv, g, 1, 1), (b, hkv, g, l, 1))
    p = jax.nn.softmax(jnp.concatenate([s, sk], axis=-1), axis=-1)[..., :l].astype(v.dtype)
    return jnp.einsum('bhgqk,bkhd->bqhgd', p, v).reshape(b, l, hq, d)


def _diff_lambda(lam_params, lam_init):
    lp = lam_params.astype(jnp.float32)
    return jnp.exp(jnp.sum(lp[0] * lp[1])) - jnp.exp(jnp.sum(lp[2] * lp[3])) + lam_init


def _merge(y_pool, mla_o, win_o, diff_o, gates, lp, lam_init):
    bsz, n = y_pool.shape[:2]
    y_mla = mla_o.reshape(bsz, n, MLA_HEADS * MLA_V) @ lp['mla_out']
    y_win = win_o.reshape(bsz, n, WIN_HEADS * WIN_HEAD_DIM) @ lp['win_out']
    diff_o = _rmsnorm(diff_o, lp['diff_subln_g']) * (1.0 - lam_init)
    y_diff = diff_o.reshape(bsz, n, DIFF_HEADS * 2 * DIFF_HEAD_DIM) @ lp['diff_out']
    merged = (gates[:, :, 0] * y_pool + gates[:, :, 1] * y_mla
              + gates[:, :, 2] * y_win + gates[:, :, 3] * y_diff)
    return merged @ lp['w_out']


def _mixer_sublayer(u, uc, pos, lp, lam_init, need_ctx):
    P = _project(u, lp, pos)
    C = _project(uc, lp, None)
    lam = _diff_lambda(lp['diff_lambda'], lam_init)
    y_pool = _pool_mixer(P['pool'], lp)
    mla_o = _dense_attn(P['mla_q'], jnp.concatenate([P['mla_k'], C['mla_k']], axis=1),
                        jnp.concatenate([P['mla_v'], C['mla_v']], axis=1), MLA_SCALE)
    win_o = _window_attn(P['win_q'], P['win_k'], P['win_v'], C['win_k'], C['win_v'], lp['win_sink'], WIN_SCALE)
    diff_o = _diff_attn(P['diff_q'], jnp.concatenate([P['diff_k'], C['diff_k']], axis=1),
                        jnp.concatenate([P['diff_v'], C['diff_v']], axis=1), lam, DIFF_SCALE)
    y = _merge(y_pool, mla_o, win_o, diff_o, P['gates'], lp, lam_init)
    if not need_ctx:
        return y, None
    yc_pool = _pool_mixer(C['pool'], lp)
    mla_c = _dense_attn(C['mla_q'], C['mla_k'], C['mla_v'], MLA_SCALE)
    win_c = _sink_attn(C['win_q'], C['win_k'], C['win_v'], lp['win_sink'], WIN_SCALE)
    diff_c = _diff_attn(C['diff_q'], C['diff_k'], C['diff_v'], lam, DIFF_SCALE)
    yc = _merge(yc_pool, mla_c, win_c, diff_c, C['gates'], lp, lam_init)
    return y, yc


def _swiglu(x, wg, wu, wd):
    return (jax.nn.silu(x @ wg) * (x @ wu)) @ wd


def _moe(x, router, router_b, wg, wu, wd):
    logits = (x @ router + router_b).astype(jnp.float32)
    top_v, top_i = lax.top_k(logits, TOP_K)
    top_w = jax.nn.softmax(top_v, axis=-1)
    gates = jnp.einsum('bnk,bnke->bne', top_w, jax.nn.one_hot(top_i, N_EXPERTS, dtype=jnp.float32)).astype(x.dtype)
    y = jnp.zeros_like(x)
    for e in range(N_EXPERTS):
        y = y + gates[..., e:e + 1] * _swiglu(x, wg[e], wu[e], wd[e])
    return y


def setup_inputs(seed: int = 0) -> dict:
    key = jax.random.key(seed)
    keys = jax.random.split(key, 32)
    f32 = jnp.float32
    D = D_MODEL
    n_dense = (DEPTH + 1) // 2
    n_moe = DEPTH // 2

    def nrm(i, shape, scale):
        return jax.random.normal(keys[i], shape, f32) * scale

    return {
        'x': nrm(0, (BATCH, SEQ, D), 1.0),
        'c': nrm(1, (BATCH, D), 1.0),
        'ctx': nrm(2, (BATCH, CTX_LEN, D), 1.0),
        'c_ctx': nrm(3, (D,), 1.0),
        'w_mod': nrm(4, (DEPTH, D, 6 * D), 0.5 * D ** -0.5),
        'b_mod': nrm(5, (DEPTH, 6 * D), 0.02),
        'norm1_g': 1.0 + nrm(6, (DEPTH, D), 0.02),
        'norm2_g': 1.0 + nrm(7, (DEPTH, D), 0.02),
        'w_in': nrm(8, (DEPTH, D, IN_COLS), D ** -0.5),
        'pool_w': nrm(9, (DEPTH, POOL_GROUPS, POOL_GROUP_DIM, POOL_GROUP_DIM), POOL_GROUP_DIM ** -0.5),
        'pool_scale': 1.0 + nrm(10, (DEPTH, POOL_WIDTH), 0.1),
        'pool_out': nrm(11, (DEPTH, POOL_WIDTH, D), POOL_WIDTH ** -0.5),
        'mla_kv_norm_g': 1.0 + nrm(12, (DEPTH, MLA_KV_RANK), 0.02),
        'mla_w_kv_b': nrm(13, (DEPTH, MLA_KV_RANK, MLA_HEADS * (MLA_NOPE + MLA_V)), MLA_KV_RANK ** -0.5),
        'mla_out': nrm(14, (DEPTH, MLA_HEADS * MLA_V, D), (MLA_HEADS * MLA_V) ** -0.5),
        'win_sink': nrm(15, (DEPTH, WIN_HEADS), 0.5),
        'win_out': nrm(16, (DEPTH, WIN_HEADS * WIN_HEAD_DIM, D), (WIN_HEADS * WIN_HEAD_DIM) ** -0.5),
        'diff_lambda': nrm(17, (DEPTH, 4, DIFF_HEAD_DIM), 0.1),
        'diff_subln_g': 1.0 + nrm(18, (DEPTH, 2 * DIFF_HEAD_DIM), 0.02),
        'diff_out': nrm(19, (DEPTH, DIFF_HEADS * 2 * DIFF_HEAD_DIM, D), (DIFF_HEADS * 2 * DIFF_HEAD_DIM) ** -0.5),
        'w_out': nrm(20, (DEPTH, D, D), D ** -0.5),
        'ffn_w_gate': nrm(21, (n_dense, D, D_FF), D ** -0.5),
        'ffn_w_up': nrm(22, (n_dense, D, D_FF), D ** -0.5),
        'ffn_w_down': nrm(23, (n_dense, D_FF, D), D_FF ** -0.5),
        'moe_router': nrm(24, (n_moe, D, N_EXPERTS), D ** -0.5),
        'moe_router_b': nrm(25, (n_moe, N_EXPERTS), 0.01),
        'moe_w_gate': nrm(26, (n_moe, N_EXPERTS, D, EXPERT_FF), D ** -0.5),
        'moe_w_up': nrm(27, (n_moe, N_EXPERTS, D, EXPERT_FF), D ** -0.5),
        'moe_w_down': nrm(28, (n_moe, N_EXPERTS, EXPERT_FF, D), EXPERT_FF ** -0.5),
        'final_norm_g': 1.0 + nrm(29, (D,), 0.02),
    }


def reference(x, c, ctx, c_ctx, w_mod, b_mod, norm1_g, norm2_g, w_in, pool_w, pool_scale, pool_out,
              mla_kv_norm_g, mla_w_kv_b, mla_out, win_sink, win_out, diff_lambda, diff_subln_g,
              diff_out, w_out, ffn_w_gate, ffn_w_up, ffn_w_down, moe_router, moe_router_b,
              moe_w_gate, moe_w_up, moe_w_down, final_norm_g):
    n = x.shape[1]
    ROWS = n // GRID_W
    rows = jnp.repeat(jnp.arange(ROWS), GRID_W)
    cols = jnp.tile(jnp.arange(GRID_W), ROWS)
    h, g = x, ctx
    for l in range(DEPTH):
        need_ctx = l < DEPTH - 1
        lam_init = 0.8 - 0.6 * math.exp(-0.3 * l)
        lp = {
            'w_in': w_in[l], 'pool_w': pool_w[l], 'pool_scale': pool_scale[l], 'pool_out': pool_out[l],
            'mla_kv_norm_g': mla_kv_norm_g[l], 'mla_w_kv_b': mla_w_kv_b[l], 'mla_out': mla_out[l],
            'win_sink': win_sink[l], 'win_out': win_out[l], 'diff_lambda': diff_lambda[l],
            'diff_subln_g': diff_subln_g[l], 'diff_out': diff_out[l], 'w_out': w_out[l],
        }
        mod = jax.nn.silu(c) @ w_mod[l] + b_mod[l]
        mod_c = jax.nn.silu(c_ctx) @ w_mod[l] + b_mod[l]
        sh1, sc1, gt1, sh2, sc2, gt2 = jnp.split(mod[:, None, :], 6, axis=-1)
        csh1, csc1, cgt1, csh2, csc2, cgt2 = jnp.split(mod_c, 6, axis=-1)
        u = _modulate(h, norm1_g[l], sh1, sc1)
        uc = _modulate(g, norm1_g[l], csh1, csc1)
        y, yc = _mixer_sublayer(u, uc, (rows, cols), lp, lam_init, need_ctx)
        h = h + gt1 * y
        j = l // 2
        if l % 2 == 0:
            ffn = lambda t, j=j: _swiglu(t, ffn_w_gate[j], ffn_w_up[j], ffn_w_down[j])
        else:
            ffn = lambda t, j=j: _moe(t, moe_router[j], moe_router_b[j], moe_w_gate[j], moe_w_up[j], moe_w_down[j])
        h = h + gt2 * ffn(_modulate(h, norm2_g[l], sh2, sc2))
        if need_ctx:
            g = g + cgt1 * yc
            g = g + cgt2 * ffn(_modulate(g, norm2_g[l], csh2, csc2))
    return _rmsnorm(h, final_norm_g)
```

```python
import functools
import math

import jax
import jax.numpy as jnp
from jax import lax
from jax.experimental import pallas as pl
from jax.experimental.pallas import tpu as pltpu

F32 = jnp.float32
BF16 = jnp.bfloat16

GRID_W = 64
ROPE_BASE = 10000.0
NORM_EPS = 1e-6
NEG_INF = -1e30
POOL_GROUPS = 4
POOL_WINDOWS = (2, 4, 8, 16)
MLA_HEADS = 8
MLA_NOPE = 128
MLA_ROPE = 64
MLA_V = 128
WIN_HEADS = 8
WIN_KV_HEADS = 2
WIN_HEAD_DIM = 128
WINDOW = 128
DIFF_HEADS = 4
DIFF_HEAD_DIM = 128
N_BRANCHES = 4
N_EXPERTS = 8
TOP_K = 2

LANES = 128
MXU_DIM = 256
VMEM_LIMIT = 56 * 1024 * 1024


def _tile(n, cap, mult):
    best = None
    for t in range(mult, min(n, cap) + 1, mult):
        if n % t == 0:
            best = t
    assert best is not None, (n, cap, mult)
    return best


def _params(sem):
    return pltpu.CompilerParams(dimension_semantics=sem, vmem_limit_bytes=VMEM_LIMIT)


def _resnorm_kernel(*refs, has_y, has_mod, emit_h):
    it = iter(refs)
    h_ref = next(it)
    y_ref = next(it) if has_y else None
    gate_ref = next(it) if has_y else None
    g_ref = next(it)
    sh_ref = next(it) if has_mod else None
    sc_ref = next(it) if has_mod else None
    hout_ref = next(it) if emit_h else None
    u_ref = next(it)
    h = h_ref[...]
    if has_y:
        h = h + gate_ref[...] * y_ref[...].astype(F32)
    if emit_h:
        hout_ref[...] = h
    ms = jnp.mean(h * h, axis=-1, keepdims=True)
    v = h * lax.rsqrt(ms + NORM_EPS) * g_ref[...]
    if has_mod:
        v = v * (1.0 + sc_ref[...]) + sh_ref[...]
    u_ref[...] = v.astype(u_ref.dtype)


def _resnorm(h, norm_g, *, rows, n_lat, y=None, gate=None, shift=None, scale=None, emit_h=False,
             out_dtype=BF16):
    B, _, D = h.shape
    ctx = rows - n_lat
    tm = _tile(math.gcd(n_lat, ctx) if ctx else n_lat, 512, 16)
    n_lat_tiles = n_lat // tm
    has_y, has_mod = y is not None, shift is not None
    row_spec = pl.BlockSpec((None, tm, D), lambda b, i: (b, i, 0))
    vec_spec = pl.BlockSpec((None, 1, D), lambda b, i: (jnp.where(i < n_lat_tiles, b, B), 0, 0))
    args, specs = [h], [row_spec]
    if has_y:
        args += [y, gate]
        specs += [row_spec, vec_spec]
    args.append(norm_g.reshape(1, D))
    specs.append(pl.BlockSpec((1, D), lambda b, i: (0, 0)))
    if has_mod:
        args += [shift, scale]
        specs += [vec_spec, vec_spec]
    out_shape, out_specs = [], []
    if emit_h:
        out_shape.append(jax.ShapeDtypeStruct((B, rows, D), F32))
        out_specs.append(row_spec)
    out_shape.append(jax.ShapeDtypeStruct((B, rows, D), out_dtype))
    out_specs.append(row_spec)
    res = pl.pallas_call(
        functools.partial(_resnorm_kernel, has_y=has_y, has_mod=has_mod, emit_h=emit_h),
        grid=(B, rows // tm), in_specs=specs, out_specs=out_specs, out_shape=out_shape,
        compiler_params=_params(("parallel", "parallel")), name="resnorm")(*args)
    return res if emit_h else res[0]


def _mm_kernel(*refs, has_g, act):
    if has_g:
        a_ref, w_ref, g_ref, o_ref = refs
        af = a_ref[...].astype(F32)
        af = af * lax.rsqrt(jnp.mean(af * af, axis=-1, keepdims=True) + NORM_EPS) * g_ref[...]
        a = af.astype(BF16)
    else:
        a_ref, w_ref, o_ref = refs
        a = a_ref[...].astype(BF16)
    acc = jnp.dot(a, w_ref[...].astype(BF16), preferred_element_type=F32)
    if act == "sigmoid":
        acc = 1.0 / (1.0 + jnp.exp(-acc))
    o_ref[...] = acc.astype(o_ref.dtype)


def _mm(a, w, *, rows, k_block=0, k_width=None, rms_g=None, act=None, out_dtype=BF16,
        tm_cap=1100, tn_cap=1024, name="mm"):
    B = a.shape[0]
    K, N = w.shape
    if k_width is None:
        assert a.shape[2] == K
    tm = _tile(rows, tm_cap, 8 if rows < 16 else 16)
    tn = _tile(N, tn_cap, MXU_DIM if N % MXU_DIM == 0 else LANES)
    args = [a, w]
    specs = [pl.BlockSpec((None, tm, K), lambda b, i, j: (b, i, k_block)),
             pl.BlockSpec((K, tn), lambda b, i, j: (0, j))]
    if rms_g is not None:
        args.append(rms_g.reshape(1, K).astype(F32))
        specs.append(pl.BlockSpec((1, K), lambda b, i, j: (0, 0)))
    return pl.pallas_call(
        functools.partial(_mm_kernel, has_g=rms_g is not None, act=act),
        grid=(B, rows // tm, N // tn), in_specs=specs,
        out_specs=pl.BlockSpec((None, tm, tn), lambda b, i, j: (b, i, j)),
        out_shape=jax.ShapeDtypeStruct((B, rows, N), out_dtype),
        compiler_params=_params(("parallel", "parallel", "arbitrary")), name=name)(*args)


def _gmm_kernel(a_ref, w_ref, o_ref):
    o_ref[...] = jnp.dot(a_ref[...], w_ref[...], preferred_element_type=F32).astype(o_ref.dtype)


def _group_mm(a, w, *, rows):
    B = a.shape[0]
    G, gd, _ = w.shape
    tm = _tile(rows, 2200, 16)
    return pl.pallas_call(
        _gmm_kernel, grid=(B, rows // tm, G),
        in_specs=[pl.BlockSpec((None, tm, gd), lambda b, i, g: (b, i, g)),
                  pl.BlockSpec((None, gd, gd), lambda b, i, g: (g, 0, 0))],
        out_specs=pl.BlockSpec((None, tm, gd), lambda b, i, g: (b, i, g)),
        out_shape=jax.ShapeDtypeStruct((B, rows, G * gd), BF16),
        compiler_params=_params(("parallel", "parallel", "arbitrary")), name="pool_mix")(a, w)


def _merge_kernel(a0, a1, a2, a3, w0, w1, w2, w3, g0, g1, g2, g3, o_ref):
    acc = None
    for a_ref, w_ref, g_ref in ((a0, w0, g0), (a1, w1, g1), (a2, w2, g2), (a3, w3, g3)):
        y = jnp.dot(a_ref[...], w_ref[...], preferred_element_type=F32) * g_ref[...].astype(F32)
        acc = y if acc is None else acc + y
    o_ref[...] = acc.astype(o_ref.dtype)


def _merge(branches, weights, gates, *, rows):
    B = branches[0].shape[0]
    N = weights[0].shape[1]
    tm = _tile(rows, 1100, 16)
    tn = _tile(N, 512, LANES)
    nj = N // tn
    a_specs = [pl.BlockSpec((None, tm, w.shape[0]), lambda b, i, j: (b, i, 0)) for w in weights]
    w_specs = [pl.BlockSpec((w.shape[0], tn), lambda b, i, j: (0, j)) for w in weights]
    g_specs = [pl.BlockSpec((None, tm, tn), lambda b, i, j, br=br: (b, i, br * nj + j))
               for br in range(N_BRANCHES)]
    return pl.pallas_call(
        _merge_kernel, grid=(B, rows // tm, nj),
        in_specs=a_specs + w_specs + g_specs,
        out_specs=pl.BlockSpec((None, tm, tn), lambda b, i, j: (b, i, j)),
        out_shape=jax.ShapeDtypeStruct((B, rows, N), BF16),
        compiler_params=_params(("parallel", "parallel", "arbitrary")),
        name="merge")(*branches, *weights, gates, gates, gates, gates)


def _ffn_kernel(*refs, has_rs, n_steps):
    if has_rs:
        a_ref, wg_ref, wu_ref, wd_ref, rs_ref, o_ref, acc_ref = refs
    else:
        a_ref, wg_ref, wu_ref, wd_ref, o_ref, acc_ref = refs
    k = pl.program_id(2)

    @pl.when(k == 0)
    def _():
        acc_ref[...] = jnp.zeros_like(acc_ref)

    a = a_ref[...]
    g = jnp.dot(a, wg_ref[...], preferred_element_type=F32)
    u = jnp.dot(a, wu_ref[...], preferred_element_type=F32)
    hid = g * (1.0 / (1.0 + jnp.exp(-g))) * u
    if has_rs:
        hid = hid * rs_ref[...]
    acc_ref[...] += jnp.dot(hid.astype(BF16), wd_ref[...], preferred_element_type=F32)

    @pl.when(k == n_steps - 1)
    def _():
        o_ref[...] = acc_ref[...].astype(o_ref.dtype)


def _ffn(a, wg, wu, wd, *, rows, row_scale=None):
    B, _, D = a.shape
    E, _, F = wg.shape
    tm = _tile(rows, 1100, 16)
    tf = _tile(F, 512, LANES)
    nf = F // tf
    n_steps = E * nf
    args = [a, wg, wu, wd]
    specs = [pl.BlockSpec((None, tm, D), lambda b, i, k: (b, i, 0)),
             pl.BlockSpec((None, D, tf), lambda b, i, k: (k // nf, 0, k % nf)),
             pl.BlockSpec((None, D, tf), lambda b, i, k: (k // nf, 0, k % nf)),
             pl.BlockSpec((None, tf, D), lambda b, i, k: (k // nf, k % nf, 0))]
    if row_scale is not None:
        args.append(row_scale)
        specs.append(pl.BlockSpec((None, None, tm, 1), lambda b, i, k: (b, k // nf, i, 0)))
    return pl.pallas_call(
        functools.partial(_ffn_kernel, has_rs=row_scale is not None, n_steps=n_steps),
        grid=(B, rows // tm, n_steps), in_specs=specs,
        out_specs=pl.BlockSpec((None, tm, D), lambda b, i, k: (b, i, 0)),
        out_shape=jax.ShapeDtypeStruct((B, rows, D), BF16),
        scratch_shapes=[pltpu.VMEM((tm, D), F32)],
        compiler_params=_params(("parallel", "parallel", "arbitrary")), name="ffn")(*args)


def _softmax_step(q, k, v, m, l, acc):
    s = lax.dot_general(q, k, (((1,), (1,)), ((), ())), preferred_element_type=F32)
    m_new = jnp.maximum(m, jnp.max(s, axis=-1, keepdims=True))
    alpha = jnp.exp(m - m_new)
    p = jnp.exp(s - m_new)
    l = alpha * l + jnp.sum(p, axis=-1, keepdims=True)
    acc = alpha * acc + jnp.dot(p.astype(BF16), v, preferred_element_type=F32)
    return m_new, l, acc


def _chunk_range(n_lat_chunks, n_chunks, n_lat_qtiles):
    i = pl.program_id(2)
    return jnp.where(i < n_lat_qtiles, 0, n_lat_chunks), n_chunks


def _mla_kernel(qn_ref, qr_ref, kn_ref, kr_ref, v_ref, o_ref, *, tk, n_lat_chunks, n_chunks,
                n_lat_qtiles):
    q = jnp.concatenate([qn_ref[...], qr_ref[...]], axis=-1)
    tq = q.shape[0]
    dv = v_ref.shape[-1]
    lo, hi = _chunk_range(n_lat_chunks, n_chunks, n_lat_qtiles)

    def step(c, carry):
        off = pl.multiple_of(c * tk, tk)
        k = jnp.concatenate([kn_ref[pl.ds(off, tk), :], kr_ref[pl.ds(off, tk), :]], axis=-1)
        return _softmax_step(q, k, v_ref[pl.ds(off, tk), :], *carry)

    init = (jnp.full((tq, 1), NEG_INF, F32), jnp.zeros((tq, 1), F32), jnp.zeros((tq, dv), F32))
    _, l, acc = lax.fori_loop(lo, hi, step, init)
    o_ref[...] = (acc / l).astype(o_ref.dtype)


def _mla_attn(z, zr, kv, *, q_rows, n_lat, cols):
    B, T, _ = z.shape
    H = MLA_HEADS
    tq = tk = math.gcd(n_lat, T - n_lat)
    kern = functools.partial(_mla_kernel, tk=tk, n_lat_chunks=n_lat // tk, n_chunks=T // tk,
                             n_lat_qtiles=n_lat // tq)
    qn0, qr0, kr0 = cols["q_nope"], cols["q_rope"], cols["k_rope"]
    return pl.pallas_call(
        kern, grid=(B, H, q_rows // tq),
        in_specs=[pl.BlockSpec((None, tq, LANES), lambda b, h, i: (b, i, qn0 + h)),
                  pl.BlockSpec((None, tq, LANES), lambda b, h, i: (b, i, qr0 + h)),
                  pl.BlockSpec((None, T, LANES), lambda b, h, i: (b, 0, h)),
                  pl.BlockSpec((None, T, LANES), lambda b, h, i: (b, 0, kr0)),
                  pl.BlockSpec((None, T, LANES), lambda b, h, i: (b, 0, H + h))],
        out_specs=pl.BlockSpec((None, tq, MLA_V), lambda b, h, i: (b, i, h)),
        out_shape=jax.ShapeDtypeStruct((B, q_rows, H * MLA_V), BF16),
        compiler_params=_params(("parallel", "parallel", "arbitrary")),
        name="mla_attn")(z, zr, kv, zr, kv)


def _diff_kernel(q_ref, k_ref, v_ref, lam_ref, g_ref, o_ref, *, tk, n_lat_chunks, n_chunks,
                 n_lat_qtiles):
    d = DIFF_HEAD_DIM
    q = q_ref[...]
    q1, q2 = q[:, :d], q[:, d:]
    tq = q.shape[0]
    dv = v_ref.shape[-1]
    lo, hi = _chunk_range(n_lat_chunks, n_chunks, n_lat_qtiles)

    def step(c, carry):
        off = pl.multiple_of(c * tk, tk)
        k = k_ref[pl.ds(off, tk), :]
        v = v_ref[pl.ds(off, tk), :]
        s1 = _softmax_step(q1, k[:, :d], v, *carry[:3])
        s2 = _softmax_step(q2, k[:, d:], v, *carry[3:])
        return s1 + s2

    one = (jnp.full((tq, 1), NEG_INF, F32), jnp.zeros((tq, 1), F32), jnp.zeros((tq, dv), F32))
    _, l1, acc1, _, l2, acc2 = lax.fori_loop(lo, hi, step, one + one)
    o = acc1 / l1 - lam_ref[...] * (acc2 / l2)
    o = o * lax.rsqrt(jnp.mean(o * o, axis=-1, keepdims=True) + NORM_EPS) * g_ref[...]
    o_ref[...] = o.astype(o_ref.dtype)


def _diff_attn(z, zr, lam_vec, g_vec, *, q_rows, n_lat, cols):
    B, T, _ = z.shape
    H = DIFF_HEADS
    dd = 2 * DIFF_HEAD_DIM
    tq = tk = math.gcd(n_lat, T - n_lat)
    kern = functools.partial(_diff_kernel, tk=tk, n_lat_chunks=n_lat // tk, n_chunks=T // tk,
                             n_lat_qtiles=n_lat // tq)
    q0, k0, v0 = cols["q_diff"], cols["k_diff"], cols["v_diff"]
    vec_spec = pl.BlockSpec((1, dd), lambda b, h, i: (0, 0))
    return pl.pallas_call(
        kern, grid=(B, H, q_rows // tq),
        in_specs=[pl.BlockSpec((None, tq, dd), lambda b, h, i: (b, i, q0 + h)),
                  pl.BlockSpec((None, T, dd), lambda b, h, i: (b, 0, k0 + h)),
                  pl.BlockSpec((None, T, dd), lambda b, h, i: (b, 0, v0 + h)),
                  vec_spec, vec_spec],
        out_specs=pl.BlockSpec((None, tq, dd), lambda b, h, i: (b, i, h)),
        out_shape=jax.ShapeDtypeStruct((B, q_rows, H * dd), BF16),
        compiler_params=_params(("parallel", "parallel", "arbitrary")),
        name="diff_attn")(zr, zr, z, lam_vec, g_vec)


def _win_kernel(q_ref, kp_ref, kc_ref, kn_ref, kx_ref, vp_ref, vc_ref, vn_ref, vx_ref, sink_ref,
                o_ref, *, n_lat):
    d = WIN_HEAD_DIM
    tq = q_ref.shape[0]
    i = pl.program_id(2)
    k = jnp.concatenate([kp_ref[...], kc_ref[...], kn_ref[...], kx_ref[...]], axis=0)
    v = jnp.concatenate([vp_ref[...], vc_ref[...], vn_ref[...], vx_ref[...]], axis=0)
    n_loc = tq + 2 * WINDOW
    nk = k.shape[0]
    r = lax.broadcasted_iota(jnp.int32, (tq, nk), 0)
    c = lax.broadcasted_iota(jnp.int32, (tq, nk), 1)
    kpos = i * tq - WINDOW + c
    rel = c - WINDOW - r
    valid = (c >= n_loc) | ((jnp.abs(rel) <= WINDOW) & (kpos >= 0) & (kpos < n_lat))
    outs = []
    for g in range(WIN_HEADS // WIN_KV_HEADS):
        q = q_ref[:, g * d:(g + 1) * d]
        s = lax.dot_general(q, k, (((1,), (1,)), ((), ())), preferred_element_type=F32)
        s = jnp.where(valid, s, NEG_INF)
        sink = sink_ref[:, g * d:g * d + 1]
        m = jnp.maximum(jnp.max(s, axis=-1, keepdims=True), sink)
        p = jnp.exp(s - m)
        l = jnp.sum(p, axis=-1, keepdims=True) + jnp.exp(sink - m)
        o = jnp.dot(p.astype(BF16), v, preferred_element_type=F32)
        outs.append(o / l)
    o_ref[...] = jnp.concatenate(outs, axis=-1).astype(o_ref.dtype)


def _win_attn(z, zr, sink_vec, *, n_lat, cols):
    B, T, _ = z.shape
    d = WIN_HEAD_DIM
    grp = WIN_HEADS // WIN_KV_HEADS
    ctx = T - n_lat
    tq = _tile(n_lat, 512, WINDOW)
    per = tq // WINDOW
    q0, k0, v0 = cols["q_win"], cols["k_win"], cols["v_win"]

    def halo(col0):
        return [pl.BlockSpec((None, WINDOW, d), lambda b, h, i: (b, jnp.maximum(i * per - 1, 0), col0 + h)),
                pl.BlockSpec((None, tq, d), lambda b, h, i: (b, i, col0 + h)),
                pl.BlockSpec((None, WINDOW, d), lambda b, h, i: (b, (i + 1) * per, col0 + h)),
                pl.BlockSpec((None, ctx, d), lambda b, h, i: (b, n_lat // ctx, col0 + h))]

    return pl.pallas_call(
        functools.partial(_win_kernel, n_lat=n_lat), grid=(B, WIN_KV_HEADS, n_lat // tq),
        in_specs=[pl.BlockSpec((None, tq, grp * d), lambda b, h, i: (b, i, q0 + h))]
        + halo(k0) + halo(v0) + [pl.BlockSpec((1, grp * d), lambda b, h, i: (0, h))],
        out_specs=pl.BlockSpec((None, tq, grp * d), lambda b, h, i: (b, i, h)),
        out_shape=jax.ShapeDtypeStruct((B, n_lat, WIN_HEADS * d), BF16),
        compiler_params=_params(("parallel", "parallel", "arbitrary")),
        name="win_attn")(zr, zr, zr, zr, zr, z, z, z, z, sink_vec)


def _sink_kernel(q_ref, k_ref, v_ref, sink_ref, o_ref):
    d = WIN_HEAD_DIM
    k = k_ref[...]
    v = v_ref[...]
    outs = []
    for g in range(WIN_HEADS // WIN_KV_HEADS):
        q = q_ref[:, g * d:(g + 1) * d]
        s = lax.dot_general(q, k, (((1,), (1,)), ((), ())), preferred_element_type=F32)
        sink = sink_ref[:, g * d:g * d + 1]
        m = jnp.maximum(jnp.max(s, axis=-1, keepdims=True), sink)
        p = jnp.exp(s - m)
        l = jnp.sum(p, axis=-1, keepdims=True) + jnp.exp(sink - m)
        outs.append(jnp.dot(p.astype(BF16), v, preferred_element_type=F32) / l)
    o_ref[...] = jnp.concatenate(outs, axis=-1).astype(o_ref.dtype)


def _sink_attn(z, zr, sink_vec, *, n_lat, cols):
    B, T, _ = z.shape
    d = WIN_HEAD_DIM
    grp = WIN_HEADS // WIN_KV_HEADS
    ctx = T - n_lat
    blk = n_lat // ctx
    q0, k0, v0 = cols["q_win"], cols["k_win"], cols["v_win"]
    return pl.pallas_call(
        _sink_kernel, grid=(B, WIN_KV_HEADS),
        in_specs=[pl.BlockSpec((None, ctx, grp * d), lambda b, h: (b, blk, q0 + h)),
                  pl.BlockSpec((None, ctx, d), lambda b, h: (b, blk, k0 + h)),
                  pl.BlockSpec((None, ctx, d), lambda b, h: (b, blk, v0 + h)),
                  pl.BlockSpec((1, grp * d), lambda b, h: (0, h))],
        out_specs=pl.BlockSpec((None, ctx, grp * d), lambda b, h: (b, 0, h)),
        out_shape=jax.ShapeDtypeStruct((B, ctx, WIN_HEADS * d), BF16),
        compiler_params=_params(("parallel", "parallel")), name="sink_attn")(zr, zr, z, sink_vec)


def _in_proj_layout(D):
    pool_w = POOL_GROUPS * (D // 8)
    widths = [("pool", pool_w), ("q_mla", MLA_HEADS * (MLA_NOPE + MLA_ROPE)), ("ckv", D // 4),
              ("k_rope", MLA_ROPE), ("q_win", WIN_HEADS * WIN_HEAD_DIM),
              ("k_win", WIN_KV_HEADS * WIN_HEAD_DIM), ("v_win", WIN_KV_HEADS * WIN_HEAD_DIM),
              ("q_diff", DIFF_HEADS * 2 * DIFF_HEAD_DIM), ("k_diff", DIFF_HEADS * 2 * DIFF_HEAD_DIM),
              ("v_diff", DIFF_HEADS * 2 * DIFF_HEAD_DIM), ("gates", N_BRANCHES * D)]
    out, off = {}, 0
    for name, w in widths:
        out[name] = (off, w)
        off += w
    return out


def _prep_in_proj(w_in, D):
    lay = _in_proj_layout(D)

    def cols(name):
        o, w = lay[name]
        return w_in[:, o:o + w]

    q_mla = cols("q_mla").reshape(D, MLA_HEADS, MLA_NOPE + MLA_ROPE) * (MLA_NOPE + MLA_ROPE) ** -0.5
    q_nope = q_mla[:, :, :MLA_NOPE].reshape(D, MLA_HEADS * MLA_NOPE)
    q_rope = jnp.pad(q_mla[:, :, MLA_NOPE:], ((0, 0), (0, 0), (0, LANES - MLA_ROPE))).reshape(D, MLA_HEADS * LANES)
    k_rope = jnp.pad(cols("k_rope"), ((0, 0), (0, LANES - MLA_ROPE)))
    plain = [cols("pool"), q_nope, cols("ckv"), cols("v_win"), cols("v_diff")]
    rot = [cols("q_win") * WIN_HEAD_DIM ** -0.5, cols("q_diff") * DIFF_HEAD_DIM ** -0.5,
           cols("k_win"), cols("k_diff"), q_rope, k_rope]
    n_plain = sum(x.shape[1] for x in plain)
    n_rot = sum(x.shape[1] for x in rot)
    total = n_plain + n_rot
    pad = (-total) % MXU_DIM
    w = jnp.concatenate(plain + rot + [jnp.zeros((D, pad), w_in.dtype)], axis=1).astype(BF16)
    return w, cols("gates").astype(BF16), n_plain, n_rot


def _rope_tables(n_lat, ctx):
    t = jnp.arange(n_lat)
    rows, cols_ = (t // GRID_W).astype(F32), (t % GRID_W).astype(F32)

    def table(half, pad):
        inv = ROPE_BASE ** (-jnp.arange(half, dtype=F32) / half)
        ar, ac = rows[:, None] * inv[None, :], cols_[:, None] * inv[None, :]
        cos = jnp.concatenate([jnp.cos(ar)] * 2 + [jnp.cos(ac)] * 2, axis=1)
        sin = jnp.concatenate([-jnp.sin(ar), jnp.sin(ar), -jnp.sin(ac), jnp.sin(ac)], axis=1)
        cos = jnp.pad(cos, ((0, ctx), (0, pad)), constant_values=1.0)
        sin = jnp.pad(sin, ((0, ctx), (0, pad)))
        return cos, sin

    return table(WIN_HEAD_DIM // 4, 0), table(MLA_ROPE // 4, LANES - MLA_ROPE)


def _apply_rope(zr, tabs, n_rot):
    B, T, _ = zr.shape
    nh = n_rot // LANES
    n64 = MLA_HEADS + 1
    x = zr.astype(F32).reshape(B, T, nh, LANES)

    def rot(x, cos, sin, half):
        xs = x.reshape(x.shape[:-1] + (LANES // (2 * half), 2, half))
        xs = jnp.flip(xs, axis=-2).reshape(x.shape)
        return x * cos[None, :, None, :] + xs * sin[None, :, None, :]

    (c128, s128), (c64, s64) = tabs
    a = rot(x[:, :, :nh - n64], c128, s128, WIN_HEAD_DIM // 4)
    b = rot(x[:, :, nh - n64:], c64, s64, MLA_ROPE // 4)
    return jnp.concatenate([a, b], axis=2).reshape(B, T, n_rot).astype(BF16)


def _pool_stencil(a, gd):
    B, n, _ = a.shape
    af = a.astype(F32)
    t = jnp.arange(n)
    outs = []
    for i, w in enumerate(POOL_WINDOWS):
        x = af[:, :, i * gd:(i + 1) * gd]
        xp = jnp.pad(x, ((0, 0), (w // 2, w // 2), (0, 0)))
        s = xp[:, 0:n]
        for dlt in range(1, w):
            s = s + xp[:, dlt:dlt + n]
        cnt = (jnp.clip(t - w // 2 + w, 0, n) - jnp.clip(t - w // 2, 0, n)).astype(F32)
        outs.append(s / cnt[None, :, None] - x)
    return jnp.concatenate(outs, axis=-1).astype(BF16)


def _layer(h, l, n_layers, P, mods, tabs, *, n_lat):
    B, T, D = h["h"].shape
    need_ctx = l < n_layers - 1
    rows = T if need_ctx else n_lat
    lam_init = 0.8 - 0.6 * math.exp(-0.3 * l)
    sh1, sc1, gt1, sh2, sc2, gt2 = mods

    if h["y"] is None:
        u = _resnorm(h["h"], P["norm1_g"][l], rows=T, n_lat=n_lat, shift=sh1, scale=sc1)
        hres = h["h"]
    else:
        hres, u = _resnorm(h["h"], P["norm1_g"][l], rows=T, n_lat=n_lat, y=h["y"], gate=h["gate"],
                           shift=sh1, scale=sc1, emit_h=True)

    w_main, w_gate, n_plain, n_rot = _prep_in_proj(P["w_in"][l], D)
    z = _mm(u, w_main, rows=T, tn_cap=1536, name="in_proj")
    gates = _mm(u, w_gate, rows=rows, act="sigmoid", name="gate_proj")
    zr = _apply_rope(z[:, :, n_plain:n_plain + n_rot], tabs, n_rot)

    gd = D // 8
    pool_w = POOL_GROUPS * gd
    kvr = D // 4
    o_qn, o_ckv = pool_w, pool_w + MLA_HEADS * MLA_NOPE
    o_vw = o_ckv + kvr
    o_vd = o_vw + WIN_KV_HEADS * WIN_HEAD_DIM
    r_qd = WIN_HEADS * WIN_HEAD_DIM
    r_kw = r_qd + DIFF_HEADS * 2 * DIFF_HEAD_DIM
    r_kd = r_kw + WIN_KV_HEADS * WIN_HEAD_DIM
    r_qr = r_kd + DIFF_HEADS * 2 * DIFF_HEAD_DIM
    r_kr = r_qr + MLA_HEADS * LANES
    dd = 2 * DIFF_HEAD_DIM
    for off, blk in ((o_qn, LANES), (o_ckv, kvr), (o_vw, LANES), (o_vd, dd), (r_qd, dd), (r_kw, LANES),
                     (r_kd, dd), (r_qr, LANES), (r_kr, LANES)):
        assert off % blk == 0, (off, blk)
    cols = {"q_nope": o_qn // LANES, "q_rope": r_qr // LANES, "k_rope": r_kr // LANES,
            "q_diff": r_qd // dd, "k_diff": r_kd // dd, "v_diff": o_vd // dd,
            "q_win": 0, "k_win": r_kw // LANES, "v_win": o_vw // LANES}

    wkv = P["mla_w_kv_b"][l].reshape(kvr, MLA_HEADS, MLA_NOPE + MLA_V)
    wkv = jnp.concatenate([wkv[:, :, :MLA_NOPE].reshape(kvr, -1), wkv[:, :, MLA_NOPE:].reshape(kvr, -1)],
                          axis=1).astype(BF16)
    kv = _mm(z, wkv, rows=T, k_block=o_ckv // kvr, k_width=kvr, rms_g=P["mla_kv_norm_g"][l], name="kv_proj")

    a_pool = z[:, :, :pool_w]
    pooled = _pool_stencil(a_pool[:, :n_lat], gd)
    if need_ctx:
        pooled = jnp.concatenate([pooled, _pool_stencil(a_pool[:, n_lat:], gd)], axis=1)
    mixed = _group_mm(pooled, P["pool_w"][l].astype(BF16), rows=rows)

    mla_o = _mla_attn(z, zr, kv, q_rows=rows, n_lat=n_lat, cols=cols)
    lp = P["diff_lambda"][l].astype(F32)
    lam = jnp.exp(jnp.sum(lp[0] * lp[1])) - jnp.exp(jnp.sum(lp[2] * lp[3])) + lam_init
    lam_vec = jnp.full((1, dd), lam, F32)
    g_vec = (P["diff_subln_g"][l].astype(F32) * (1.0 - lam_init)).reshape(1, dd)
    diff_o = _diff_attn(z, zr, lam_vec, g_vec, q_rows=rows, n_lat=n_lat, cols=cols)
    sink_vec = jnp.repeat(P["win_sink"][l].astype(F32), WIN_HEAD_DIM).reshape(1, WIN_HEADS * WIN_HEAD_DIM)
    win_o = _win_attn(z, zr, sink_vec, n_lat=n_lat, cols=cols)
    if need_ctx:
        win_c = _sink_attn(z, zr, sink_vec, n_lat=n_lat, cols=cols)
        win_o = jnp.concatenate([win_o, win_c], axis=1)

    w_pool_out = (P["pool_scale"][l][:, None] * P["pool_out"][l]).astype(BF16)
    merged = _merge([mixed, mla_o, win_o, diff_o],
                    [w_pool_out, P["mla_out"][l].astype(BF16), P["win_out"][l].astype(BF16),
                     P["diff_out"][l].astype(BF16)], gates, rows=rows)
    y1 = _mm(merged, P["w_out"][l].astype(BF16), rows=rows, name="out_proj")

    h2, u2 = _resnorm(hres, P["norm2_g"][l], rows=rows, n_lat=n_lat, y=y1, gate=gt1, shift=sh2, scale=sc2,
                      emit_h=True)
    j = l // 2
    if l % 2 == 0:
        y2 = _ffn(u2, P["ffn_w_gate"][j][None].astype(BF16), P["ffn_w_up"][j][None].astype(BF16),
                  P["ffn_w_down"][j][None].astype(BF16), rows=rows)
    else:
        router = jnp.pad(P["moe_router"][j], ((0, 0), (0, LANES - N_EXPERTS)))
        logits = _mm(u2, router, rows=rows, out_dtype=F32, name="router")[:, :, :N_EXPERTS] + P["moe_router_b"][j]
        top_v, top_i = lax.top_k(logits, TOP_K)
        top_w = jax.nn.softmax(top_v, axis=-1)
        eg = jnp.einsum("bnk,bnke->bne", top_w, jax.nn.one_hot(top_i, N_EXPERTS, dtype=F32))
        row_scale = jnp.transpose(eg, (0, 2, 1))[..., None]
        y2 = _ffn(u2, P["moe_w_gate"][j].astype(BF16), P["moe_w_up"][j].astype(BF16),
                  P["moe_w_down"][j].astype(BF16), rows=rows, row_scale=row_scale)
    return {"h": h2, "y": y2, "gate": gt2}


def kernel(x, c, ctx, c_ctx, w_mod, b_mod, norm1_g, norm2_g, w_in, pool_w, pool_scale, pool_out,
           mla_kv_norm_g, mla_w_kv_b, mla_out, win_sink, win_out, diff_lambda, diff_subln_g,
           diff_out, w_out, ffn_w_gate, ffn_w_up, ffn_w_down, moe_router, moe_router_b,
           moe_w_gate, moe_w_up, moe_w_down, final_norm_g):
    B, n_lat, D = x.shape
    n_ctx = ctx.shape[1]
    n_layers = w_in.shape[0]
    P = dict(norm1_g=norm1_g, norm2_g=norm2_g, w_in=w_in, pool_w=pool_w, pool_scale=pool_scale,
             pool_out=pool_out, mla_kv_norm_g=mla_kv_norm_g, mla_w_kv_b=mla_w_kv_b, mla_out=mla_out,
             win_sink=win_sink, win_out=win_out, diff_lambda=diff_lambda, diff_subln_g=diff_subln_g,
             diff_out=diff_out, w_out=w_out, ffn_w_gate=ffn_w_gate, ffn_w_up=ffn_w_up,
             ffn_w_down=ffn_w_down, moe_router=moe_router, moe_router_b=moe_router_b,
             moe_w_gate=moe_w_gate, moe_w_up=moe_w_up, moe_w_down=moe_w_down)
    tabs = _rope_tables(n_lat, n_ctx)

    cond = jnp.concatenate([c, c_ctx[None]], axis=0)
    cond = cond * (1.0 / (1.0 + jnp.exp(-cond)))
    n_pad = (-cond.shape[0]) % 8
    cond = jnp.pad(cond, ((0, n_pad), (0, 0)))[None]

    state = {"h": jnp.concatenate([x, ctx], axis=1), "y": None, "gate": None}
    for l in range(n_layers):
        mod = _mm(cond, w_mod[l], rows=cond.shape[1], out_dtype=F32, name="mod")[0, :B + 1] + b_mod[l]
        mods = [m.reshape(B + 1, 1, D) for m in jnp.split(mod, 6, axis=-1)]
        state = _layer(state, l, n_layers, P, mods, tabs, n_lat=n_lat)
    return _resnorm(state["h"], final_norm_g, rows=n_lat, n_lat=n_lat, y=state["y"], gate=state["gate"],
                    out_dtype=F32)
```

```python
import functools
import math

import jax
import jax.numpy as jnp
from jax import lax
from jax.experimental import pallas as pl
from jax.experimental.pallas import tpu as pltpu

F32 = jnp.float32
BF16 = jnp.bfloat16

GRID_W = 64
ROPE_BASE = 10000.0
NORM_EPS = 1e-6
NEG_INF = -1e30
POOL_GROUPS = 4
POOL_WINDOWS = (2, 4, 8, 16)
MLA_HEADS = 8
MLA_NOPE = 128
MLA_ROPE = 64
MLA_V = 128
WIN_HEADS = 8
WIN_KV_HEADS = 2
WIN_HEAD_DIM = 128
WINDOW = 128
DIFF_HEADS = 4
DIFF_HEAD_DIM = 128
N_BRANCHES = 4
N_EXPERTS = 8
TOP_K = 2

LOG2E = math.log2(math.e)
LANES = 128
MXU_DIM = 256
VMEM_LIMIT = 56 * 1024 * 1024


def _tile(n, cap, mult):
    best = None
    for t in range(mult, min(n, cap) + 1, mult):
        if n % t == 0:
            best = t
    assert best is not None, (n, cap, mult)
    return best


def _params(sem):
    return pltpu.CompilerParams(dimension_semantics=sem, vmem_limit_bytes=VMEM_LIMIT)


def _resnorm_kernel(*refs, has_y, has_mod, emit_h):
    it = iter(refs)
    h_ref = next(it)
    y_ref = next(it) if has_y else None
    gate_ref = next(it) if has_y else None
    g_ref = next(it)
    sh_ref = next(it) if has_mod else None
    sc_ref = next(it) if has_mod else None
    hout_ref = next(it) if emit_h else None
    u_ref = next(it)
    h = h_ref[...]
    if has_y:
        h = h + gate_ref[...] * y_ref[...].astype(F32)
    if emit_h:
        hout_ref[...] = h
    ms = jnp.mean(h * h, axis=-1, keepdims=True)
    v = h * lax.rsqrt(ms + NORM_EPS) * g_ref[...]
    if has_mod:
        v = v * (1.0 + sc_ref[...]) + sh_ref[...]
    u_ref[...] = v.astype(u_ref.dtype)


def _resnorm(h, norm_g, *, rows, n_lat, y=None, gate=None, shift=None, scale=None, emit_h=False,
             out_dtype=BF16):
    B, _, D = h.shape
    ctx = rows - n_lat
    tm = _tile(math.gcd(n_lat, ctx) if ctx else n_lat, 512, 16)
    n_lat_tiles = n_lat // tm
    has_y, has_mod = y is not None, shift is not None
    row_spec = pl.BlockSpec((None, tm, D), lambda b, i: (b, i, 0))
    vec_spec = pl.BlockSpec((None, 1, D), lambda b, i: (jnp.where(i < n_lat_tiles, b, B), 0, 0))
    args, specs = [h], [row_spec]
    if has_y:
        args += [y, gate]
        specs += [row_spec, vec_spec]
    args.append(norm_g.reshape(1, D))
    specs.append(pl.BlockSpec((1, D), lambda b, i: (0, 0)))
    if has_mod:
        args += [shift, scale]
        specs += [vec_spec, vec_spec]
    out_shape, out_specs = [], []
    if emit_h:
        out_shape.append(jax.ShapeDtypeStruct((B, rows, D), F32))
        out_specs.append(row_spec)
    out_shape.append(jax.ShapeDtypeStruct((B, rows, D), out_dtype))
    out_specs.append(row_spec)
    res = pl.pallas_call(
        functools.partial(_resnorm_kernel, has_y=has_y, has_mod=has_mod, emit_h=emit_h),
        grid=(B, rows // tm), in_specs=specs, out_specs=out_specs, out_shape=out_shape,
        compiler_params=_params(("parallel", "parallel")), name="resnorm")(*args)
    return res if emit_h else res[0]


def _mm_kernel(*refs, has_g, act):
    if has_g:
        a_ref, w_ref, g_ref, o_ref = refs
        af = a_ref[...].astype(F32)
        af = af * lax.rsqrt(jnp.mean(af * af, axis=-1, keepdims=True) + NORM_EPS) * g_ref[...]
        a = af.astype(BF16)
    else:
        a_ref, w_ref, o_ref = refs
        a = a_ref[...].astype(BF16)
    acc = jnp.dot(a, w_ref[...].astype(BF16), preferred_element_type=F32)
    if act == "sigmoid":
        acc = 1.0 / (1.0 + jnp.exp(-acc))
    o_ref[...] = acc.astype(o_ref.dtype)


def _mm(a, w, *, rows, k_block=0, k_width=None, rms_g=None, act=None, out_dtype=BF16,
        tm_cap=1100, tn_cap=1024, name="mm"):
    B = a.shape[0]
    K, N = w.shape
    if k_width is None:
        assert a.shape[2] == K
    tm = _tile(rows, tm_cap, 8 if rows < 16 else 16)
    tn = _tile(N, tn_cap, MXU_DIM if N % MXU_DIM == 0 else LANES)
    args = [a, w]
    specs = [pl.BlockSpec((None, tm, K), lambda b, i, j: (b, i, k_block)),
             pl.BlockSpec((K, tn), lambda b, i, j: (0, j))]
    if rms_g is not None:
        args.append(rms_g.reshape(1, K).astype(F32))
        specs.append(pl.BlockSpec((1, K), lambda b, i, j: (0, 0)))
    return pl.pallas_call(
        functools.partial(_mm_kernel, has_g=rms_g is not None, act=act),
        grid=(B, rows // tm, N // tn), in_specs=specs,
        out_specs=pl.BlockSpec((None, tm, tn), lambda b, i, j: (b, i, j)),
        out_shape=jax.ShapeDtypeStruct((B, rows, N), out_dtype),
        compiler_params=_params(("parallel", "parallel", "arbitrary")), name=name)(*args)


def _gmm_kernel(a_ref, w_ref, o_ref):
    o_ref[...] = jnp.dot(a_ref[...], w_ref[...], preferred_element_type=F32).astype(o_ref.dtype)


def _group_mm(a, w, *, rows):
    B = a.shape[0]
    G, gd, _ = w.shape
    tm = _tile(rows, 2200, 16)
    return pl.pallas_call(
        _gmm_kernel, grid=(B, rows // tm, G),
        in_specs=[pl.BlockSpec((None, tm, gd), lambda b, i, g: (b, i, g)),
                  pl.BlockSpec((None, gd, gd), lambda b, i, g: (g, 0, 0))],
        out_specs=pl.BlockSpec((None, tm, gd), lambda b, i, g: (b, i, g)),
        out_shape=jax.ShapeDtypeStruct((B, rows, G * gd), BF16),
        compiler_params=_params(("parallel", "parallel", "arbitrary")), name="pool_mix")(a, w)


def _merge_kernel(a0, a1, a2, a3, w0, w1, w2, w3, g0, g1, g2, g3, o_ref):
    acc = None
    for a_ref, w_ref, g_ref in ((a0, w0, g0), (a1, w1, g1), (a2, w2, g2), (a3, w3, g3)):
        y = jnp.dot(a_ref[...], w_ref[...], preferred_element_type=F32) * g_ref[...].astype(F32)
        acc = y if acc is None else acc + y
    o_ref[...] = acc.astype(o_ref.dtype)


def _merge(branches, weights, gates, *, rows):
    B = branches[0].shape[0]
    N = weights[0].shape[1]
    tm = _tile(rows, 1100, 16)
    tn = _tile(N, 512, LANES)
    nj = N // tn
    a_specs = [pl.BlockSpec((None, tm, w.shape[0]), lambda b, i, j: (b, i, 0)) for w in weights]
    w_specs = [pl.BlockSpec((w.shape[0], tn), lambda b, i, j: (0, j)) for w in weights]
    g_specs = [pl.BlockSpec((None, tm, tn), lambda b, i, j, br=br: (b, i, br * nj + j))
               for br in range(N_BRANCHES)]
    return pl.pallas_call(
        _merge_kernel, grid=(B, rows // tm, nj),
        in_specs=a_specs + w_specs + g_specs,
        out_specs=pl.BlockSpec((None, tm, tn), lambda b, i, j: (b, i, j)),
        out_shape=jax.ShapeDtypeStruct((B, rows, N), BF16),
        compiler_params=_params(("parallel", "parallel", "arbitrary")),
        name="merge")(*branches, *weights, gates, gates, gates, gates)


def _ffn_kernel(*refs, has_rs, n_steps):
    if has_rs:
        a_ref, wg_ref, wu_ref, wd_ref, rs_ref, o_ref, acc_ref = refs
    else:
        a_ref, wg_ref, wu_ref, wd_ref, o_ref, acc_ref = refs
    k = pl.program_id(2)

    @pl.when(k == 0)
    def _():
        acc_ref[...] = jnp.zeros_like(acc_ref)

    a = a_ref[...]
    g = jnp.dot(a, wg_ref[...], preferred_element_type=F32)
    u = jnp.dot(a, wu_ref[...], preferred_element_type=F32)
    hid = g * (1.0 / (1.0 + jnp.exp(-g))) * u
    if has_rs:
        hid = hid * rs_ref[...]
    acc_ref[...] += jnp.dot(hid.astype(BF16), wd_ref[...], preferred_element_type=F32)

    @pl.when(k == n_steps - 1)
    def _():
        o_ref[...] = acc_ref[...].astype(o_ref.dtype)


def _ffn(a, wg, wu, wd, *, rows, row_scale=None):
    B, _, D = a.shape
    E, _, F = wg.shape
    tm = _tile(rows, 1100, 16)
    tf = _tile(F, 512, LANES)
    nf = F // tf
    n_steps = E * nf
    args = [a, wg, wu, wd]
    specs = [pl.BlockSpec((None, tm, D), lambda b, i, k: (b, i, 0)),
             pl.BlockSpec((None, D, tf), lambda b, i, k: (k // nf, 0, k % nf)),
             pl.BlockSpec((None, D, tf), lambda b, i, k: (k // nf, 0, k % nf)),
             pl.BlockSpec((None, tf, D), lambda b, i, k: (k // nf, k % nf, 0))]
    if row_scale is not None:
        args.append(row_scale)
        specs.append(pl.BlockSpec((None, None, tm, 1), lambda b, i, k: (b, k // nf, i, 0)))
    return pl.pallas_call(
        functools.partial(_ffn_kernel, has_rs=row_scale is not None, n_steps=n_steps),
        grid=(B, rows // tm, n_steps), in_specs=specs,
        out_specs=pl.BlockSpec((None, tm, D), lambda b, i, k: (b, i, 0)),
        out_shape=jax.ShapeDtypeStruct((B, rows, D), BF16),
        scratch_shapes=[pltpu.VMEM((tm, D), F32)],
        compiler_params=_params(("parallel", "parallel", "arbitrary")), name="ffn")(*args)


ATTN_CHAIN_ROWS = 256


def _softmax_rows(s):
    p = jnp.exp2(s - jnp.max(s, axis=-1, keepdims=True))
    return p.astype(BF16), jnp.sum(p, axis=-1, keepdims=True)


def _chains(tq):
    r = min(tq, ATTN_CHAIN_ROWS)
    return [slice(a * r, (a + 1) * r) for a in range(tq // r)]


def _qk(q, k):
    return lax.dot_general(q, k, (((1,), (1,)), ((), ())), preferred_element_type=F32)


def _mla_kernel(qn_ref, qr_ref, kn_ref, kr_ref, v_ref, *rest):
    o_ref = rest[-1]
    k = jnp.concatenate([kn_ref[...], kr_ref[...]], axis=-1)
    v = v_ref[...]
    for rows in _chains(qn_ref.shape[0]):
        q = jnp.concatenate([qn_ref[rows, :], qr_ref[rows, :]], axis=-1)
        p, l = _softmax_rows(_qk(q, k))
        o_ref[rows, :] = (jnp.dot(p, v, preferred_element_type=F32) / l).astype(o_ref.dtype)


def _attn_calls(kernel, make_specs, args, out_width, vec_args, *, B, H, T, n_lat, need_ctx, tq_cap, name):
    ctx = T - n_lat
    out_rows = T if need_ctx else n_lat

    def call(tq, q_blk0, nq, k_rows, k_blk, prev):
        specs = make_specs(tq, q_blk0, k_rows, k_blk)
        specs += [pl.BlockSpec(v.shape, lambda b, h, i: (0, 0)) for v in vec_args]
        ins = list(args) + list(vec_args)
        aliases = {}
        if prev is not None:
            specs.append(pl.BlockSpec(memory_space=pl.ANY))
            aliases = {len(ins): 0}
            ins.append(prev)
        return pl.pallas_call(
            kernel, grid=(B, H, nq), in_specs=specs,
            out_specs=pl.BlockSpec((None, tq, out_width), lambda b, h, i: (b, q_blk0 + i, h)),
            out_shape=jax.ShapeDtypeStruct((B, out_rows, H * out_width), BF16),
            input_output_aliases=aliases,
            compiler_params=_params(("parallel", "parallel", "arbitrary")), name=name)(*ins)

    tq = _tile(n_lat, tq_cap, ATTN_CHAIN_ROWS)
    out = call(tq, 0, n_lat // tq, T, 0, None)
    if need_ctx:
        out = call(ctx, n_lat // ctx, 1, ctx, n_lat // ctx, out)
    return out


def _mla_attn(z, zr, kv, *, n_lat, need_ctx, cols):
    B, T, _ = z.shape
    H = MLA_HEADS
    qn0, qr0, kr0 = cols["q_nope"], cols["q_rope"], cols["k_rope"]

    def specs(tq, q_blk0, k_rows, k_blk):
        return [pl.BlockSpec((None, tq, LANES), lambda b, h, i: (b, q_blk0 + i, qn0 + h)),
                pl.BlockSpec((None, tq, LANES), lambda b, h, i: (b, q_blk0 + i, qr0 + h)),
                pl.BlockSpec((None, k_rows, LANES), lambda b, h, i: (b, k_blk, h)),
                pl.BlockSpec((None, k_rows, LANES), lambda b, h, i: (b, k_blk, kr0)),
                pl.BlockSpec((None, k_rows, LANES), lambda b, h, i: (b, k_blk, H + h))]

    return _attn_calls(_mla_kernel, specs, (z, zr, kv, zr, kv), MLA_V, (), B=B, H=H, T=T, n_lat=n_lat,
                       need_ctx=need_ctx, tq_cap=1024, name="mla_attn")


def _diff_kernel(q_ref, k_ref, v_ref, lam_ref, g_ref, *rest):
    o_ref = rest[-1]
    d = DIFF_HEAD_DIM
    k = k_ref[...]
    k1, k2 = k[:, :d], k[:, d:]
    v = v_ref[...]
    for rows in _chains(q_ref.shape[0]):
        q = q_ref[rows, :]
        p1, l1 = _softmax_rows(_qk(q[:, :d], k1))
        p2, l2 = _softmax_rows(_qk(q[:, d:], k2))
        o = (jnp.dot(p1, v, preferred_element_type=F32) / l1
             - lam_ref[...] * (jnp.dot(p2, v, preferred_element_type=F32) / l2))
        o = o * lax.rsqrt(jnp.mean(o * o, axis=-1, keepdims=True) + NORM_EPS) * g_ref[...]
        o_ref[rows, :] = o.astype(o_ref.dtype)


def _diff_attn(z, zr, lam_vec, g_vec, *, n_lat, need_ctx, cols):
    B, T, _ = z.shape
    dd = 2 * DIFF_HEAD_DIM
    q0, k0, v0 = cols["q_diff"], cols["k_diff"], cols["v_diff"]

    def specs(tq, q_blk0, k_rows, k_blk):
        return [pl.BlockSpec((None, tq, dd), lambda b, h, i: (b, q_blk0 + i, q0 + h)),
                pl.BlockSpec((None, k_rows, dd), lambda b, h, i: (b, k_blk, k0 + h)),
                pl.BlockSpec((None, k_rows, dd), lambda b, h, i: (b, k_blk, v0 + h))]

    return _attn_calls(_diff_kernel, specs, (zr, zr, z), dd, (lam_vec, g_vec), B=B, H=DIFF_HEADS, T=T,
                       n_lat=n_lat, need_ctx=need_ctx, tq_cap=512, name="diff_attn")


def _win_kernel(q_ref, kp_ref, kc_ref, kn_ref, kx_ref, vp_ref, vc_ref, vn_ref, vx_ref, sink_ref,
                o_ref, *, n_lat):
    d = WIN_HEAD_DIM
    tq = q_ref.shape[0]
    i = pl.program_id(2)
    k = jnp.concatenate([kp_ref[...], kc_ref[...], kn_ref[...], kx_ref[...]], axis=0)
    v = jnp.concatenate([vp_ref[...], vc_ref[...], vn_ref[...], vx_ref[...]], axis=0)
    n_loc = tq + 2 * WINDOW
    nk = k.shape[0]
    r = lax.broadcasted_iota(jnp.int32, (tq, nk), 0)
    c = lax.broadcasted_iota(jnp.int32, (tq, nk), 1)
    kpos = i * tq - WINDOW + c
    rel = c - WINDOW - r
    valid = (c >= n_loc) | ((jnp.abs(rel) <= WINDOW) & (kpos >= 0) & (kpos < n_lat))
    outs = []
    for g in range(WIN_HEADS // WIN_KV_HEADS):
        q = q_ref[:, g * d:(g + 1) * d]
        s = lax.dot_general(q, k, (((1,), (1,)), ((), ())), preferred_element_type=F32)
        s = jnp.where(valid, s, NEG_INF)
        sink = sink_ref[:, g * d:g * d + 1]
        m = jnp.maximum(jnp.max(s, axis=-1, keepdims=True), sink)
        p = jnp.exp2(s - m)
        l = jnp.sum(p, axis=-1, keepdims=True) + jnp.exp2(sink - m)
        o = jnp.dot(p.astype(BF16), v, preferred_element_type=F32)
        outs.append(o / l)
    o_ref[...] = jnp.concatenate(outs, axis=-1).astype(o_ref.dtype)


def _win_attn(z, zr, sink_vec, *, n_lat, out_rows, cols):
    B, T, _ = z.shape
    d = WIN_HEAD_DIM
    grp = WIN_HEADS // WIN_KV_HEADS
    ctx = T - n_lat
    tq = _tile(n_lat, 512, WINDOW)
    per = tq // WINDOW
    q0, k0, v0 = cols["q_win"], cols["k_win"], cols["v_win"]

    def halo(col0):
        return [pl.BlockSpec((None, WINDOW, d), lambda b, h, i: (b, jnp.maximum(i * per - 1, 0), col0 + h)),
                pl.BlockSpec((None, tq, d), lambda b, h, i: (b, i, col0 + h)),
                pl.BlockSpec((None, WINDOW, d), lambda b, h, i: (b, (i + 1) * per, col0 + h)),
                pl.BlockSpec((None, ctx, d), lambda b, h, i: (b, n_lat // ctx, col0 + h))]

    return pl.pallas_call(
        functools.partial(_win_kernel, n_lat=n_lat), grid=(B, WIN_KV_HEADS, n_lat // tq),
        in_specs=[pl.BlockSpec((None, tq, grp * d), lambda b, h, i: (b, i, q0 + h))]
        + halo(k0) + halo(v0) + [pl.BlockSpec((1, grp * d), lambda b, h, i: (0, h))],
        out_specs=pl.BlockSpec((None, tq, grp * d), lambda b, h, i: (b, i, h)),
        out_shape=jax.ShapeDtypeStruct((B, out_rows, WIN_HEADS * d), BF16),
        compiler_params=_params(("parallel", "parallel", "arbitrary")),
        name="win_attn")(zr, zr, zr, zr, zr, z, z, z, z, sink_vec)


def _sink_kernel(q_ref, k_ref, v_ref, sink_ref, prev_ref, o_ref):
    del prev_ref
    d = WIN_HEAD_DIM
    k = k_ref[...]
    v = v_ref[...]
    outs = []
    for g in range(WIN_HEADS // WIN_KV_HEADS):
        q = q_ref[:, g * d:(g + 1) * d]
        s = lax.dot_general(q, k, (((1,), (1,)), ((), ())), preferred_element_type=F32)
        sink = sink_ref[:, g * d:g * d + 1]
        m = jnp.maximum(jnp.max(s, axis=-1, keepdims=True), sink)
        p = jnp.exp2(s - m)
        l = jnp.sum(p, axis=-1, keepdims=True) + jnp.exp2(sink - m)
        outs.append(jnp.dot(p.astype(BF16), v, preferred_element_type=F32) / l)
    o_ref[...] = jnp.concatenate(outs, axis=-1).astype(o_ref.dtype)


def _sink_attn(z, zr, sink_vec, prev, *, n_lat, cols):
    B, T, _ = z.shape
    d = WIN_HEAD_DIM
    grp = WIN_HEADS // WIN_KV_HEADS
    ctx = T - n_lat
    blk = n_lat // ctx
    q0, k0, v0 = cols["q_win"], cols["k_win"], cols["v_win"]
    return pl.pallas_call(
        _sink_kernel, grid=(B, WIN_KV_HEADS),
        in_specs=[pl.BlockSpec((None, ctx, grp * d), lambda b, h: (b, blk, q0 + h)),
                  pl.BlockSpec((None, ctx, d), lambda b, h: (b, blk, k0 + h)),
                  pl.BlockSpec((None, ctx, d), lambda b, h: (b, blk, v0 + h)),
                  pl.BlockSpec((1, grp * d), lambda b, h: (0, h)),
                  pl.BlockSpec(memory_space=pl.ANY)],
        out_specs=pl.BlockSpec((None, ctx, grp * d), lambda b, h: (b, blk, h)),
        out_shape=jax.ShapeDtypeStruct(prev.shape, prev.dtype),
        input_output_aliases={4: 0},
        compiler_params=_params(("parallel", "parallel")), name="sink_attn")(zr, zr, z, sink_vec, prev)


def _in_proj_layout(D):
    pool_w = POOL_GROUPS * (D // 8)
    widths = [("pool", pool_w), ("q_mla", MLA_HEADS * (MLA_NOPE + MLA_ROPE)), ("ckv", D // 4),
              ("k_rope", MLA_ROPE), ("q_win", WIN_HEADS * WIN_HEAD_DIM),
              ("k_win", WIN_KV_HEADS * WIN_HEAD_DIM), ("v_win", WIN_KV_HEADS * WIN_HEAD_DIM),
              ("q_diff", DIFF_HEADS * 2 * DIFF_HEAD_DIM), ("k_diff", DIFF_HEADS * 2 * DIFF_HEAD_DIM),
              ("v_diff", DIFF_HEADS * 2 * DIFF_HEAD_DIM), ("gates", N_BRANCHES * D)]
    out, off = {}, 0
    for name, w in widths:
        out[name] = (off, w)
        off += w
    return out


def _prep_in_proj(w_in, D):
    lay = _in_proj_layout(D)

    def cols(name):
        o, w = lay[name]
        return w_in[:, o:o + w]

    q_mla = cols("q_mla").reshape(D, MLA_HEADS, MLA_NOPE + MLA_ROPE) * (LOG2E * (MLA_NOPE + MLA_ROPE) ** -0.5)
    q_nope = q_mla[:, :, :MLA_NOPE].reshape(D, MLA_HEADS * MLA_NOPE)
    q_rope = jnp.pad(q_mla[:, :, MLA_NOPE:], ((0, 0), (0, 0), (0, LANES - MLA_ROPE))).reshape(D, MLA_HEADS * LANES)
    k_rope = jnp.pad(cols("k_rope"), ((0, 0), (0, LANES - MLA_ROPE)))
    plain = [cols("pool"), q_nope, cols("ckv"), cols("v_win"), cols("v_diff")]
    rot = [cols("q_win") * (LOG2E * WIN_HEAD_DIM ** -0.5), cols("q_diff") * (LOG2E * DIFF_HEAD_DIM ** -0.5),
           cols("k_win"), cols("k_diff"), q_rope, k_rope]
    n_plain = sum(x.shape[1] for x in plain)
    n_rot = sum(x.shape[1] for x in rot)
    total = n_plain + n_rot
    pad = (-total) % MXU_DIM
    w = jnp.concatenate(plain + rot + [jnp.zeros((D, pad), w_in.dtype)], axis=1).astype(BF16)
    return w, cols("gates").astype(BF16), n_plain, n_rot


def _rope_tables(n_lat, ctx):
    t = jnp.arange(n_lat)
    rows, cols_ = (t // GRID_W).astype(F32), (t % GRID_W).astype(F32)

    def table(half, pad):
        inv = ROPE_BASE ** (-jnp.arange(half, dtype=F32) / half)
        ar, ac = rows[:, None] * inv[None, :], cols_[:, None] * inv[None, :]
        cos = jnp.concatenate([jnp.cos(ar)] * 2 + [jnp.cos(ac)] * 2, axis=1)
        sin = jnp.concatenate([-jnp.sin(ar), jnp.sin(ar), -jnp.sin(ac), jnp.sin(ac)], axis=1)
        cos = jnp.pad(cos, ((0, ctx), (0, pad)), constant_values=1.0)
        sin = jnp.pad(sin, ((0, ctx), (0, pad)))
        return cos, sin

    return table(WIN_HEAD_DIM // 4, 0), table(MLA_ROPE // 4, LANES - MLA_ROPE)


def _apply_rope(zr, tabs, n_rot):
    B, T, _ = zr.shape
    nh = n_rot // LANES
    n64 = MLA_HEADS + 1
    x = zr.astype(F32).reshape(B, T, nh, LANES)

    def rot(x, cos, sin, half):
        xs = x.reshape(x.shape[:-1] + (LANES // (2 * half), 2, half))
        xs = jnp.flip(xs, axis=-2).reshape(x.shape)
        return x * cos[None, :, None, :] + xs * sin[None, :, None, :]

    (c128, s128), (c64, s64) = tabs
    a = rot(x[:, :, :nh - n64], c128, s128, WIN_HEAD_DIM // 4)
    b = rot(x[:, :, nh - n64:], c64, s64, MLA_ROPE // 4)
    return jnp.concatenate([a, b], axis=2).reshape(B, T, n_rot).astype(BF16)


def _pool_stencil(a, gd):
    B, n, _ = a.shape
    af = a.astype(F32)
    t = jnp.arange(n)
    outs = []
    for i, w in enumerate(POOL_WINDOWS):
        x = af[:, :, i * gd:(i + 1) * gd]
        xp = jnp.pad(x, ((0, 0), (w // 2, w // 2), (0, 0)))
        s = xp[:, 0:n]
        for dlt in range(1, w):
            s = s + xp[:, dlt:dlt + n]
        cnt = (jnp.clip(t - w // 2 + w, 0, n) - jnp.clip(t - w // 2, 0, n)).astype(F32)
        outs.append(s / cnt[None, :, None] - x)
    return jnp.concatenate(outs, axis=-1).astype(BF16)


def _layer(h, l, n_layers, P, mods, tabs, *, n_lat):
    B, T, D = h["h"].shape
    need_ctx = l < n_layers - 1
    rows = T if need_ctx else n_lat
    lam_init = 0.8 - 0.6 * math.exp(-0.3 * l)
    sh1, sc1, gt1, sh2, sc2, gt2 = mods

    if h["y"] is None:
        u = _resnorm(h["h"], P["norm1_g"][l], rows=T, n_lat=n_lat, shift=sh1, scale=sc1)
        hres = h["h"]
    else:
        hres, u = _resnorm(h["h"], P["norm1_g"][l], rows=T, n_lat=n_lat, y=h["y"], gate=h["gate"],
                           shift=sh1, scale=sc1, emit_h=True)

    w_main, w_gate, n_plain, n_rot = _prep_in_proj(P["w_in"][l], D)
    z = _mm(u, w_main, rows=T, tn_cap=1536, name="in_proj")
    gates = _mm(u, w_gate, rows=rows, act="sigmoid", name="gate_proj")
    zr = _apply_rope(z[:, :, n_plain:n_plain + n_rot], tabs, n_rot)

    gd = D // 8
    pool_w = POOL_GROUPS * gd
    kvr = D // 4
    o_qn, o_ckv = pool_w, pool_w + MLA_HEADS * MLA_NOPE
    o_vw = o_ckv + kvr
    o_vd = o_vw + WIN_KV_HEADS * WIN_HEAD_DIM
    r_qd = WIN_HEADS * WIN_HEAD_DIM
    r_kw = r_qd + DIFF_HEADS * 2 * DIFF_HEAD_DIM
    r_kd = r_kw + WIN_KV_HEADS * WIN_HEAD_DIM
    r_qr = r_kd + DIFF_HEADS * 2 * DIFF_HEAD_DIM
    r_kr = r_qr + MLA_HEADS * LANES
    dd = 2 * DIFF_HEAD_DIM
    for off, blk in ((o_qn, LANES), (o_ckv, kvr), (o_vw, LANES), (o_vd, dd), (r_qd, dd), (r_kw, LANES),
                     (r_kd, dd), (r_qr, LANES), (r_kr, LANES)):
        assert off % blk == 0, (off, blk)
    cols = {"q_nope": o_qn // LANES, "q_rope": r_qr // LANES, "k_rope": r_kr // LANES,
            "q_diff": r_qd // dd, "k_diff": r_kd // dd, "v_diff": o_vd // dd,
            "q_win": 0, "k_win": r_kw // LANES, "v_win": o_vw // LANES}

    wkv = P["mla_w_kv_b"][l].reshape(kvr, MLA_HEADS, MLA_NOPE + MLA_V)
    wkv = jnp.concatenate([wkv[:, :, :MLA_NOPE].reshape(kvr, -1), wkv[:, :, MLA_NOPE:].reshape(kvr, -1)],
                          axis=1).astype(BF16)
    kv = _mm(z, wkv, rows=T, k_block=o_ckv // kvr, k_width=kvr, rms_g=P["mla_kv_norm_g"][l], name="kv_proj")

    a_pool = z[:, :, :pool_w]
    pooled = _pool_stencil(a_pool[:, :n_lat], gd)
    if need_ctx:
        pooled = jnp.concatenate([pooled, _pool_stencil(a_pool[:, n_lat:], gd)], axis=1)
    mixed = _group_mm(pooled, P["pool_w"][l].astype(BF16), rows=rows)

    mla_o = _mla_attn(z, zr, kv, n_lat=n_lat, need_ctx=need_ctx, cols=cols)
    lp = P["diff_lambda"][l].astype(F32)
    lam = jnp.exp(jnp.sum(lp[0] * lp[1])) - jnp.exp(jnp.sum(lp[2] * lp[3])) + lam_init
    lam_vec = jnp.full((1, dd), lam, F32)
    g_vec = (P["diff_subln_g"][l].astype(F32) * (1.0 - lam_init)).reshape(1, dd)
    diff_o = _diff_attn(z, zr, lam_vec, g_vec, n_lat=n_lat, need_ctx=need_ctx, cols=cols)
    sink_vec = jnp.repeat(P["win_sink"][l].astype(F32) * LOG2E, WIN_HEAD_DIM).reshape(1, WIN_HEADS * WIN_HEAD_DIM)
    win_o = _win_attn(z, zr, sink_vec, n_lat=n_lat, out_rows=rows, cols=cols)
    if need_ctx:
        win_o = _sink_attn(z, zr, sink_vec, win_o, n_lat=n_lat, cols=cols)

    w_pool_out = (P["pool_scale"][l][:, None] * P["pool_out"][l]).astype(BF16)
    merged = _merge([mixed, mla_o, win_o, diff_o],
                    [w_pool_out, P["mla_out"][l].astype(BF16), P["win_out"][l].astype(BF16),
                     P["diff_out"][l].astype(BF16)], gates, rows=rows)
    y1 = _mm(merged, P["w_out"][l].astype(BF16), rows=rows, name="out_proj")

    h2, u2 = _resnorm(hres, P["norm2_g"][l], rows=rows, n_lat=n_lat, y=y1, gate=gt1, shift=sh2, scale=sc2,
                      emit_h=True)
    j = l // 2
    if l % 2 == 0:
        y2 = _ffn(u2, P["ffn_w_gate"][j][None].astype(BF16), P["ffn_w_up"][j][None].astype(BF16),
                  P["ffn_w_down"][j][None].astype(BF16), rows=rows)
    else:
        router = jnp.pad(P["moe_router"][j], ((0, 0), (0, LANES - N_EXPERTS)))
        logits = _mm(u2, router, rows=rows, out_dtype=F32, name="router")[:, :, :N_EXPERTS] + P["moe_router_b"][j]
        top_v, top_i = lax.top_k(logits, TOP_K)
        top_w = jax.nn.softmax(top_v, axis=-1)
        eg = jnp.einsum("bnk,bnke->bne", top_w, jax.nn.one_hot(top_i, N_EXPERTS, dtype=F32))
        row_scale = jnp.transpose(eg, (0, 2, 1))[..., None]
        y2 = _ffn(u2, P["moe_w_gate"][j].astype(BF16), P["moe_w_up"][j].astype(BF16),
                  P["moe_w_down"][j].astype(BF16), rows=rows, row_scale=row_scale)
    return {"h": h2, "y": y2, "gate": gt2}


def kernel(x, c, ctx, c_ctx, w_mod, b_mod, norm1_g, norm2_g, w_in, pool_w, pool_scale, pool_out,
           mla_kv_norm_g, mla_w_kv_b, mla_out, win_sink, win_out, diff_lambda, diff_subln_g,
           diff_out, w_out, ffn_w_gate, ffn_w_up, ffn_w_down, moe_router, moe_router_b,
           moe_w_gate, moe_w_up, moe_w_down, final_norm_g):
    B, n_lat, D = x.shape
    n_ctx = ctx.shape[1]
    n_layers = w_in.shape[0]
    P = dict(norm1_g=norm1_g, norm2_g=norm2_g, w_in=w_in, pool_w=pool_w, pool_scale=pool_scale,
             pool_out=pool_out, mla_kv_norm_g=mla_kv_norm_g, mla_w_kv_b=mla_w_kv_b, mla_out=mla_out,
             win_sink=win_sink, win_out=win_out, diff_lambda=diff_lambda, diff_subln_g=diff_subln_g,
             diff_out=diff_out, w_out=w_out, ffn_w_gate=ffn_w_gate, ffn_w_up=ffn_w_up,
             ffn_w_down=ffn_w_down, moe_router=moe_router, moe_router_b=moe_router_b,
             moe_w_gate=moe_w_gate, moe_w_up=moe_w_up, moe_w_down=moe_w_down)
    tabs = _rope_tables(n_lat, n_ctx)

    cond = jnp.concatenate([c, c_ctx[None]], axis=0)
    cond = cond * (1.0 / (1.0 + jnp.exp(-cond)))
    n_pad = (-cond.shape[0]) % 8
    cond = jnp.pad(cond, ((0, n_pad), (0, 0)))[None]

    state = {"h": jnp.concatenate([x, ctx], axis=1), "y": None, "gate": None}
    for l in range(n_layers):
        mod = _mm(cond, w_mod[l], rows=cond.shape[1], out_dtype=F32, name="mod")[0, :B + 1] + b_mod[l]
        mods = [m.reshape(B + 1, 1, D) for m in jnp.split(mod, 6, axis=-1)]
        state = _layer(state, l, n_layers, P, mods, tabs, n_lat=n_lat)
    return _resnorm(state["h"], final_norm_g, rows=n_lat, n_lat=n_lat, y=state["y"], gate=state["gate"],
                    out_dtype=F32)
```

```python
import functools
import math

import jax
import jax.numpy as jnp
from jax import lax
from jax.experimental import pallas as pl
from jax.experimental.pallas import tpu as pltpu

F32 = jnp.float32
BF16 = jnp.bfloat16

GRID_W = 64
ROPE_BASE = 10000.0
NORM_EPS = 1e-6
NEG_INF = -1e30
POOL_GROUPS = 4
POOL_WINDOWS = (2, 4, 8, 16)
MLA_HEADS = 8
MLA_NOPE = 128
MLA_ROPE = 64
MLA_V = 128
WIN_HEADS = 8
WIN_KV_HEADS = 2
WIN_HEAD_DIM = 128
WINDOW = 128
DIFF_HEADS = 4
DIFF_HEAD_DIM = 128
N_BRANCHES = 4
N_EXPERTS = 8
TOP_K = 2

LOG2E = math.log2(math.e)
LANES = 128
MXU_DIM = 256
VMEM_LIMIT = 56 * 1024 * 1024


def _tile(n, cap, mult):
    best = None
    for t in range(mult, min(n, cap) + 1, mult):
        if n % t == 0:
            best = t
    assert best is not None, (n, cap, mult)
    return best


def _params(sem):
    return pltpu.CompilerParams(dimension_semantics=sem, vmem_limit_bytes=VMEM_LIMIT)


def _resnorm_kernel(*refs, has_y, n_slots, has_mod, emit_h):
    it = iter(refs)
    h_ref = next(it)
    y_refs = [next(it) for _ in range(max(n_slots, 1))] if has_y else []
    yw_ref = next(it) if n_slots else None
    gate_ref = next(it) if has_y else None
    g_ref = next(it)
    sh_ref = next(it) if has_mod else None
    sc_ref = next(it) if has_mod else None
    hout_ref = next(it) if emit_h else None
    u_ref = next(it)
    h = h_ref[...]
    if has_y:
        if n_slots:
            y = sum(yw_ref[:, k:k + 1] * y_refs[k][...] for k in range(n_slots))
        else:
            y = y_refs[0][...].astype(F32)
        h = h + gate_ref[...] * y
    if emit_h:
        hout_ref[...] = h
    ms = jnp.mean(h * h, axis=-1, keepdims=True)
    v = h * lax.rsqrt(ms + NORM_EPS) * g_ref[...]
    if has_mod:
        v = v * (1.0 + sc_ref[...]) + sh_ref[...]
    u_ref[...] = v.astype(u_ref.dtype)


def _resnorm(h, norm_g, *, rows, n_lat, y=None, y_w=None, gate=None, shift=None, scale=None, emit_h=False,
             out_dtype=BF16):
    B, _, D = h.shape
    ctx = rows - n_lat
    tm = _tile(math.gcd(n_lat, ctx) if ctx else n_lat, 512, 16)
    n_lat_tiles = n_lat // tm
    has_y, has_mod = y is not None, shift is not None
    n_slots = 0 if y_w is None else y_w.shape[-1]
    row_spec = pl.BlockSpec((None, tm, D), lambda b, i: (b, i, 0))
    vec_spec = pl.BlockSpec((None, 1, D), lambda b, i: (jnp.where(i < n_lat_tiles, b, B), 0, 0))
    args, specs = [h], [row_spec]
    if n_slots:
        per_b = rows // tm
        args += [y] * n_slots + [y_w, gate]
        specs += [pl.BlockSpec((tm, D), lambda b, i, k=k: ((k * B + b) * per_b + i, 0)) for k in range(n_slots)]
        specs += [pl.BlockSpec((None, tm, n_slots), lambda b, i: (b, i, 0)), vec_spec]
    elif has_y:
        args += [y, gate]
        specs += [row_spec, vec_spec]
    args.append(norm_g.reshape(1, D))
    specs.append(pl.BlockSpec((1, D), lambda b, i: (0, 0)))
    if has_mod:
        args += [shift, scale]
        specs += [vec_spec, vec_spec]
    out_shape, out_specs = [], []
    if emit_h:
        out_shape.append(jax.ShapeDtypeStruct((B, rows, D), F32))
        out_specs.append(row_spec)
    out_shape.append(jax.ShapeDtypeStruct((B, rows, D), out_dtype))
    out_specs.append(row_spec)
    res = pl.pallas_call(
        functools.partial(_resnorm_kernel, has_y=has_y, n_slots=n_slots, has_mod=has_mod, emit_h=emit_h),
        grid=(B, rows // tm), in_specs=specs, out_specs=out_specs, out_shape=out_shape,
        compiler_params=_params(("parallel", "parallel")), name="resnorm")(*args)
    return res if emit_h else res[0]


def _mm_kernel(*refs, has_g, act):
    if has_g:
        a_ref, w_ref, g_ref, o_ref = refs
        af = a_ref[...].astype(F32)
        af = af * lax.rsqrt(jnp.mean(af * af, axis=-1, keepdims=True) + NORM_EPS) * g_ref[...]
        a = af.astype(BF16)
    else:
        a_ref, w_ref, o_ref = refs
        a = a_ref[...].astype(BF16)
    acc = jnp.dot(a, w_ref[...].astype(BF16), preferred_element_type=F32)
    if act == "sigmoid":
        acc = 1.0 / (1.0 + jnp.exp(-acc))
    o_ref[...] = acc.astype(o_ref.dtype)


def _mm(a, w, *, rows, k_block=0, k_width=None, rms_g=None, act=None, out_dtype=BF16,
        tm_cap=1100, tn_cap=1024, name="mm"):
    B = a.shape[0]
    K, N = w.shape
    if k_width is None:
        assert a.shape[2] == K
    tm = _tile(rows, tm_cap, 8 if rows < 16 else 16)
    tn = _tile(N, tn_cap, MXU_DIM if N % MXU_DIM == 0 else LANES)
    args = [a, w]
    specs = [pl.BlockSpec((None, tm, K), lambda b, i, j: (b, i, k_block)),
             pl.BlockSpec((K, tn), lambda b, i, j: (0, j))]
    if rms_g is not None:
        args.append(rms_g.reshape(1, K).astype(F32))
        specs.append(pl.BlockSpec((1, K), lambda b, i, j: (0, 0)))
    return pl.pallas_call(
        functools.partial(_mm_kernel, has_g=rms_g is not None, act=act),
        grid=(B, rows // tm, N // tn), in_specs=specs,
        out_specs=pl.BlockSpec((None, tm, tn), lambda b, i, j: (b, i, j)),
        out_shape=jax.ShapeDtypeStruct((B, rows, N), out_dtype),
        compiler_params=_params(("parallel", "parallel", "arbitrary")), name=name)(*args)


def _gmm_kernel(a_ref, w_ref, o_ref):
    o_ref[...] = jnp.dot(a_ref[...], w_ref[...], preferred_element_type=F32).astype(o_ref.dtype)


def _group_mm(a, w, *, rows):
    B = a.shape[0]
    G, gd, _ = w.shape
    tm = _tile(rows, 2200, 16)
    return pl.pallas_call(
        _gmm_kernel, grid=(B, rows // tm, G),
        in_specs=[pl.BlockSpec((None, tm, gd), lambda b, i, g: (b, i, g)),
                  pl.BlockSpec((None, gd, gd), lambda b, i, g: (g, 0, 0))],
        out_specs=pl.BlockSpec((None, tm, gd), lambda b, i, g: (b, i, g)),
        out_shape=jax.ShapeDtypeStruct((B, rows, G * gd), BF16),
        compiler_params=_params(("parallel", "parallel", "arbitrary")), name="pool_mix")(a, w)


def _merge_kernel(a0, a1, a2, a3, w0, w1, w2, w3, g0, g1, g2, g3, o_ref):
    acc = None
    for a_ref, w_ref, g_ref in ((a0, w0, g0), (a1, w1, g1), (a2, w2, g2), (a3, w3, g3)):
        y = jnp.dot(a_ref[...], w_ref[...], preferred_element_type=F32) * g_ref[...].astype(F32)
        acc = y if acc is None else acc + y
    o_ref[...] = acc.astype(o_ref.dtype)


def _merge(branches, weights, gates, *, rows):
    B = branches[0].shape[0]
    N = weights[0].shape[1]
    tm = _tile(rows, 1100, 16)
    tn = _tile(N, 512, LANES)
    nj = N // tn
    a_specs = [pl.BlockSpec((None, tm, w.shape[0]), lambda b, i, j: (b, i, 0)) for w in weights]
    w_specs = [pl.BlockSpec((w.shape[0], tn), lambda b, i, j: (0, j)) for w in weights]
    g_specs = [pl.BlockSpec((None, tm, tn), lambda b, i, j, br=br: (b, i, br * nj + j))
               for br in range(N_BRANCHES)]
    return pl.pallas_call(
        _merge_kernel, grid=(B, rows // tm, nj),
        in_specs=a_specs + w_specs + g_specs,
        out_specs=pl.BlockSpec((None, tm, tn), lambda b, i, j: (b, i, j)),
        out_shape=jax.ShapeDtypeStruct((B, rows, N), BF16),
        compiler_params=_params(("parallel", "parallel", "arbitrary")),
        name="merge")(*branches, *weights, gates, gates, gates, gates)


def _ffn_kernel(*refs, has_rs, n_steps):
    if has_rs:
        a_ref, wg_ref, wu_ref, wd_ref, rs_ref, o_ref, acc_ref = refs
    else:
        a_ref, wg_ref, wu_ref, wd_ref, o_ref, acc_ref = refs
    k = pl.program_id(2)

    @pl.when(k == 0)
    def _():
        acc_ref[...] = jnp.zeros_like(acc_ref)

    a = a_ref[...]
    g = jnp.dot(a, wg_ref[...], preferred_element_type=F32)
    u = jnp.dot(a, wu_ref[...], preferred_element_type=F32)
    hid = g * (1.0 / (1.0 + jnp.exp(-g))) * u
    if has_rs:
        hid = hid * rs_ref[...]
    acc_ref[...] += jnp.dot(hid.astype(BF16), wd_ref[...], preferred_element_type=F32)

    @pl.when(k == n_steps - 1)
    def _():
        o_ref[...] = acc_ref[...].astype(o_ref.dtype)


def _ffn(a, wg, wu, wd, *, rows, row_scale=None):
    B, _, D = a.shape
    E, _, F = wg.shape
    tm = _tile(rows, 1100, 16)
    tf = _tile(F, 512, LANES)
    nf = F // tf
    n_steps = E * nf
    args = [a, wg, wu, wd]
    specs = [pl.BlockSpec((None, tm, D), lambda b, i, k: (b, i, 0)),
             pl.BlockSpec((None, D, tf), lambda b, i, k: (k // nf, 0, k % nf)),
             pl.BlockSpec((None, D, tf), lambda b, i, k: (k // nf, 0, k % nf)),
             pl.BlockSpec((None, tf, D), lambda b, i, k: (k // nf, k % nf, 0))]
    if row_scale is not None:
        args.append(row_scale)
        specs.append(pl.BlockSpec((None, None, tm, 1), lambda b, i, k: (b, k // nf, i, 0)))
    return pl.pallas_call(
        functools.partial(_ffn_kernel, has_rs=row_scale is not None, n_steps=n_steps),
        grid=(B, rows // tm, n_steps), in_specs=specs,
        out_specs=pl.BlockSpec((None, tm, D), lambda b, i, k: (b, i, 0)),
        out_shape=jax.ShapeDtypeStruct((B, rows, D), BF16),
        scratch_shapes=[pltpu.VMEM((tm, D), F32)],
        compiler_params=_params(("parallel", "parallel", "arbitrary")), name="ffn")(*args)


def _moe_kernel(te_ref, nu_ref, x_hbm, ord_hbm, wg_ref, wu_ref, wd_ref, y_hbm,
                gbuf, xb, acc, idx, sem_g, sem_s, sem_i, *, tm, nf, n_tok):
    del te_ref
    t = pl.program_id(0)
    f = pl.program_id(1)
    n_used = nu_ref[0]
    slot = t % 2

    def idx_copy(tile, s):
        return pltpu.make_async_copy(ord_hbm.at[pl.ds(tile * tm, tm)], idx.at[s], sem_i)

    def gather_copy(tok, r):
        return pltpu.make_async_copy(x_hbm.at[pl.ds(tok, 1)], gbuf.at[pl.ds(r, 1)], sem_g)

    def scatter_copy(s, r, dst):
        return pltpu.make_async_copy(acc.at[s, pl.ds(r, 1)], y_hbm.at[pl.ds(dst, 1)], sem_s.at[s])

    def issue_gather(s):
        def body(r, c):
            gather_copy(jnp.maximum(idx[s, r], 0) // TOP_K, r).start()
            return c
        lax.fori_loop(0, tm, body, 0, unroll=DMA_UNROLL)

    def wait_gather():
        pltpu.make_async_copy(x_hbm.at[pl.ds(0, tm)], gbuf, sem_g).wait()

    def issue_scatter(s):
        def body(r, c):
            a = idx[s, r]
            dst = jnp.where(a >= 0, (a % TOP_K) * n_tok + a // TOP_K, TOP_K * n_tok + r)
            scatter_copy(s, r, dst).start()
            return c
        lax.fori_loop(0, tm, body, 0, unroll=DMA_UNROLL)

    def wait_scatter(s):
        pltpu.make_async_copy(acc.at[s], y_hbm.at[pl.ds(0, tm)], sem_s.at[s]).wait()

    @pl.when(t < n_used)
    def _():
        @pl.when(f == 0)
        def _():
            @pl.when(t == 0)
            def _():
                first = idx_copy(0, 0)
                first.start()
                first.wait()
                issue_gather(0)

            wait_gather()
            xb[...] = gbuf[...].astype(BF16)

            @pl.when(t + 1 < n_used)
            def _():
                idx_copy(t + 1, 1 - slot).start()

        @pl.when((f == 1) & (t + 1 < n_used))
        def _():
            idx_copy(t + 1, 1 - slot).wait()
            issue_gather(1 - slot)

        a = xb[...]
        g = jnp.dot(a, wg_ref[...], preferred_element_type=F32)
        u = jnp.dot(a, wu_ref[...], preferred_element_type=F32)
        hid = (g * (1.0 / (1.0 + jnp.exp(-g))) * u).astype(BF16)
        part = jnp.dot(hid, wd_ref[...], preferred_element_type=F32)

        @pl.when(f == 0)
        def _():
            acc[slot] = part

        @pl.when(f > 0)
        def _():
            acc[slot] += part

        @pl.when(f == nf - 1)
        def _():
            issue_scatter(slot)

            @pl.when(t > 0)
            def _():
                wait_scatter(1 - slot)

            @pl.when(t == n_used - 1)
            def _():
                wait_scatter(slot)


MOE_TILE_ROWS = 1024
DMA_UNROLL = 8


def _moe_route(logits, tm):
    n_tok = logits.shape[0]
    top_v, top_i = lax.top_k(logits, TOP_K)
    top_w = jax.nn.softmax(top_v, axis=-1)
    e_flat = top_i.reshape(-1).astype(jnp.int32)
    n_asg = e_flat.shape[0]
    cnt = jnp.sum(jax.nn.one_hot(e_flat, N_EXPERTS, dtype=jnp.int32), axis=0)
    pad = (-cnt) % tm
    big = 2 * N_EXPERTS
    j = jnp.arange(tm, dtype=jnp.int32)[None, :]
    e = jnp.arange(N_EXPERTS, dtype=jnp.int32)[:, None]
    pad_keys = jnp.where(j < pad[:, None], 2 * e + 1, big).reshape(-1)
    keys = jnp.concatenate([2 * e_flat, pad_keys])
    vals = jnp.concatenate([jnp.arange(n_asg, dtype=jnp.int32), jnp.full((N_EXPERTS * tm,), -1, jnp.int32)])
    keys, order = lax.sort_key_val(keys, vals)
    n_tiles = keys.shape[0] // tm
    tile_key = keys.reshape(n_tiles, tm)[:, 0]
    used = tile_key < big
    n_used = jnp.sum(used).astype(jnp.int32)
    tile_e = jnp.where(used, tile_key // 2, 0)
    tile_e = jnp.where(used, tile_e, tile_e[jnp.maximum(n_used - 1, 0)])
    return top_w, order, tile_e.astype(jnp.int32), n_used.reshape(1)


def _moe(x, logits, wg, wu, wd):
    n_tok, D = x.shape
    E, _, F = wg.shape
    tm = _tile(n_tok, MOE_TILE_ROWS, 16)
    tf = _tile(F, 512, LANES)
    nf = F // tf
    assert nf >= 2
    top_w, order, tile_e, n_used = _moe_route(logits, tm)
    n_tiles = order.shape[0] // tm

    def w_in_map(t, f, te, nu):
        return (te[t], 0, jnp.where(t < nu[0], f, nf - 1))

    def w_out_map(t, f, te, nu):
        return (te[t], jnp.where(t < nu[0], f, nf - 1), 0)

    grid_spec = pltpu.PrefetchScalarGridSpec(
        num_scalar_prefetch=2, grid=(n_tiles, nf),
        in_specs=[pl.BlockSpec(memory_space=pl.ANY), pl.BlockSpec(memory_space=pl.ANY),
                  pl.BlockSpec((None, D, tf), w_in_map), pl.BlockSpec((None, D, tf), w_in_map),
                  pl.BlockSpec((None, tf, D), w_out_map)],
        out_specs=pl.BlockSpec(memory_space=pl.ANY),
        scratch_shapes=[pltpu.VMEM((tm, D), F32), pltpu.VMEM((tm, D), BF16), pltpu.VMEM((2, tm, D), F32),
                        pltpu.SMEM((2, tm), jnp.int32), pltpu.SemaphoreType.DMA(()),
                        pltpu.SemaphoreType.DMA((2,)), pltpu.SemaphoreType.DMA(())])
    y = pl.pallas_call(
        functools.partial(_moe_kernel, tm=tm, nf=nf, n_tok=n_tok), grid_spec=grid_spec,
        out_shape=jax.ShapeDtypeStruct((TOP_K * n_tok + tm, D), F32),
        compiler_params=_params(("arbitrary", "arbitrary")), name="moe")(
            tile_e, n_used, x, order, wg, wu, wd)
    return y, top_w


ATTN_CHAIN_ROWS = 256


def _softmax_rows(s):
    p = jnp.exp2(s - jnp.max(s, axis=-1, keepdims=True))
    return p.astype(BF16), jnp.sum(p, axis=-1, keepdims=True)


def _chains(tq):
    r = min(tq, ATTN_CHAIN_ROWS)
    return [slice(a * r, (a + 1) * r) for a in range(tq // r)]


def _qk(q, k):
    return lax.dot_general(q, k, (((1,), (1,)), ((), ())), preferred_element_type=F32)


def _mla_kernel(qn_ref, qr_ref, kn_ref, kr_ref, v_ref, *rest):
    o_ref = rest[-1]
    k = jnp.concatenate([kn_ref[...], kr_ref[...]], axis=-1)
    v = v_ref[...]
    for rows in _chains(qn_ref.shape[0]):
        q = jnp.concatenate([qn_ref[rows, :], qr_ref[rows, :]], axis=-1)
        p, l = _softmax_rows(_qk(q, k))
        o_ref[rows, :] = (jnp.dot(p, v, preferred_element_type=F32) / l).astype(o_ref.dtype)


def _attn_calls(kernel, make_specs, args, out_width, vec_args, *, B, H, T, n_lat, need_ctx, tq_cap, name):
    ctx = T - n_lat
    out_rows = T if need_ctx else n_lat

    def call(tq, q_blk0, nq, k_rows, k_blk, prev):
        specs = make_specs(tq, q_blk0, k_rows, k_blk)
        specs += [pl.BlockSpec(v.shape, lambda b, h, i: (0, 0)) for v in vec_args]
        ins = list(args) + list(vec_args)
        aliases = {}
        if prev is not None:
            specs.append(pl.BlockSpec(memory_space=pl.ANY))
            aliases = {len(ins): 0}
            ins.append(prev)
        return pl.pallas_call(
            kernel, grid=(B, H, nq), in_specs=specs,
            out_specs=pl.BlockSpec((None, tq, out_width), lambda b, h, i: (b, q_blk0 + i, h)),
            out_shape=jax.ShapeDtypeStruct((B, out_rows, H * out_width), BF16),
            input_output_aliases=aliases,
            compiler_params=_params(("parallel", "parallel", "arbitrary")), name=name)(*ins)

    tq = _tile(n_lat, tq_cap, ATTN_CHAIN_ROWS)
    out = call(tq, 0, n_lat // tq, T, 0, None)
    if need_ctx:
        out = call(ctx, n_lat // ctx, 1, ctx, n_lat // ctx, out)
    return out


def _mla_attn(z, zr, kv, *, n_lat, need_ctx, cols):
    B, T, _ = z.shape
    H = MLA_HEADS
    qn0, qr0, kr0 = cols["q_nope"], cols["q_rope"], cols["k_rope"]

    def specs(tq, q_blk0, k_rows, k_blk):
        return [pl.BlockSpec((None, tq, LANES), lambda b, h, i: (b, q_blk0 + i, qn0 + h)),
                pl.BlockSpec((None, tq, LANES), lambda b, h, i: (b, q_blk0 + i, qr0 + h)),
                pl.BlockSpec((None, k_rows, LANES), lambda b, h, i: (b, k_blk, h)),
                pl.BlockSpec((None, k_rows, LANES), lambda b, h, i: (b, k_blk, kr0)),
                pl.BlockSpec((None, k_rows, LANES), lambda b, h, i: (b, k_blk, H + h))]

    return _attn_calls(_mla_kernel, specs, (z, zr, kv, zr, kv), MLA_V, (), B=B, H=H, T=T, n_lat=n_lat,
                       need_ctx=need_ctx, tq_cap=1024, name="mla_attn")


def _diff_kernel(q_ref, k_ref, v_ref, lam_ref, g_ref, *rest):
    o_ref = rest[-1]
    d = DIFF_HEAD_DIM
    k = k_ref[...]
    k1, k2 = k[:, :d], k[:, d:]
    v = v_ref[...]
    for rows in _chains(q_ref.shape[0]):
        q = q_ref[rows, :]
        p1, l1 = _softmax_rows(_qk(q[:, :d], k1))
        p2, l2 = _softmax_rows(_qk(q[:, d:], k2))
        o = (jnp.dot(p1, v, preferred_element_type=F32) / l1
             - lam_ref[...] * (jnp.dot(p2, v, preferred_element_type=F32) / l2))
        o = o * lax.rsqrt(jnp.mean(o * o, axis=-1, keepdims=True) + NORM_EPS) * g_ref[...]
        o_ref[rows, :] = o.astype(o_ref.dtype)


def _diff_attn(z, zr, lam_vec, g_vec, *, n_lat, need_ctx, cols):
    B, T, _ = z.shape
    dd = 2 * DIFF_HEAD_DIM
    q0, k0, v0 = cols["q_diff"], cols["k_diff"], cols["v_diff"]

    def specs(tq, q_blk0, k_rows, k_blk):
        return [pl.BlockSpec((None, tq, dd), lambda b, h, i: (b, q_blk0 + i, q0 + h)),
                pl.BlockSpec((None, k_rows, dd), lambda b, h, i: (b, k_blk, k0 + h)),
                pl.BlockSpec((None, k_rows, dd), lambda b, h, i: (b, k_blk, v0 + h))]

    return _attn_calls(_diff_kernel, specs, (zr, zr, z), dd, (lam_vec, g_vec), B=B, H=DIFF_HEADS, T=T,
                       n_lat=n_lat, need_ctx=need_ctx, tq_cap=512, name="diff_attn")


def _win_kernel(q_ref, kp_ref, kc_ref, kn_ref, kx_ref, vp_ref, vc_ref, vn_ref, vx_ref, sink_ref,
                o_ref, *, n_lat):
    d = WIN_HEAD_DIM
    tq = q_ref.shape[0]
    i = pl.program_id(2)
    k = jnp.concatenate([kp_ref[...], kc_ref[...], kn_ref[...], kx_ref[...]], axis=0)
    v = jnp.concatenate([vp_ref[...], vc_ref[...], vn_ref[...], vx_ref[...]], axis=0)
    n_loc = tq + 2 * WINDOW
    nk = k.shape[0]
    r = lax.broadcasted_iota(jnp.int32, (tq, nk), 0)
    c = lax.broadcasted_iota(jnp.int32, (tq, nk), 1)
    kpos = i * tq - WINDOW + c
    rel = c - WINDOW - r
    valid = (c >= n_loc) | ((jnp.abs(rel) <= WINDOW) & (kpos >= 0) & (kpos < n_lat))
    outs = []
    for g in range(WIN_HEADS // WIN_KV_HEADS):
        q = q_ref[:, g * d:(g + 1) * d]
        s = lax.dot_general(q, k, (((1,), (1,)), ((), ())), preferred_element_type=F32)
        s = jnp.where(valid, s, NEG_INF)
        sink = sink_ref[:, g * d:g * d + 1]
        m = jnp.maximum(jnp.max(s, axis=-1, keepdims=True), sink)
        p = jnp.exp2(s - m)
        l = jnp.sum(p, axis=-1, keepdims=True) + jnp.exp2(sink - m)
        o = jnp.dot(p.astype(BF16), v, preferred_element_type=F32)
        outs.append(o / l)
    o_ref[...] = jnp.concatenate(outs, axis=-1).astype(o_ref.dtype)


def _win_attn(z, zr, sink_vec, *, n_lat, out_rows, cols):
    B, T, _ = z.shape
    d = WIN_HEAD_DIM
    grp = WIN_HEADS // WIN_KV_HEADS
    ctx = T - n_lat
    tq = _tile(n_lat, 512, WINDOW)
    per = tq // WINDOW
    q0, k0, v0 = cols["q_win"], cols["k_win"], cols["v_win"]

    def halo(col0):
        return [pl.BlockSpec((None, WINDOW, d), lambda b, h, i: (b, jnp.maximum(i * per - 1, 0), col0 + h)),
                pl.BlockSpec((None, tq, d), lambda b, h, i: (b, i, col0 + h)),
                pl.BlockSpec((None, WINDOW, d), lambda b, h, i: (b, (i + 1) * per, col0 + h)),
                pl.BlockSpec((None, ctx, d), lambda b, h, i: (b, n_lat // ctx, col0 + h))]

    return pl.pallas_call(
        functools.partial(_win_kernel, n_lat=n_lat), grid=(B, WIN_KV_HEADS, n_lat // tq),
        in_specs=[pl.BlockSpec((None, tq, grp * d), lambda b, h, i: (b, i, q0 + h))]
        + halo(k0) + halo(v0) + [pl.BlockSpec((1, grp * d), lambda b, h, i: (0, h))],
        out_specs=pl.BlockSpec((None, tq, grp * d), lambda b, h, i: (b, i, h)),
        out_shape=jax.ShapeDtypeStruct((B, out_rows, WIN_HEADS * d), BF16),
        compiler_params=_params(("parallel", "parallel", "arbitrary")),
        name="win_attn")(zr, zr, zr, zr, zr, z, z, z, z, sink_vec)


def _sink_kernel(q_ref, k_ref, v_ref, sink_ref, prev_ref, o_ref):
    del prev_ref
    d = WIN_HEAD_DIM
    k = k_ref[...]
    v = v_ref[...]
    outs = []
    for g in range(WIN_HEADS // WIN_KV_HEADS):
        q = q_ref[:, g * d:(g + 1) * d]
        s = lax.dot_general(q, k, (((1,), (1,)), ((), ())), preferred_element_type=F32)
        sink = sink_ref[:, g * d:g * d + 1]
        m = jnp.maximum(jnp.max(s, axis=-1, keepdims=True), sink)
        p = jnp.exp2(s - m)
        l = jnp.sum(p, axis=-1, keepdims=True) + jnp.exp2(sink - m)
        outs.append(jnp.dot(p.astype(BF16), v, preferred_element_type=F32) / l)
    o_ref[...] = jnp.concatenate(outs, axis=-1).astype(o_ref.dtype)


def _sink_attn(z, zr, sink_vec, prev, *, n_lat, cols):
    B, T, _ = z.shape
    d = WIN_HEAD_DIM
    grp = WIN_HEADS // WIN_KV_HEADS
    ctx = T - n_lat
    blk = n_lat // ctx
    q0, k0, v0 = cols["q_win"], cols["k_win"], cols["v_win"]
    return pl.pallas_call(
        _sink_kernel, grid=(B, WIN_KV_HEADS),
        in_specs=[pl.BlockSpec((None, ctx, grp * d), lambda b, h: (b, blk, q0 + h)),
                  pl.BlockSpec((None, ctx, d), lambda b, h: (b, blk, k0 + h)),
                  pl.BlockSpec((None, ctx, d), lambda b, h: (b, blk, v0 + h)),
                  pl.BlockSpec((1, grp * d), lambda b, h: (0, h)),
                  pl.BlockSpec(memory_space=pl.ANY)],
        out_specs=pl.BlockSpec((None, ctx, grp * d), lambda b, h: (b, blk, h)),
        out_shape=jax.ShapeDtypeStruct(prev.shape, prev.dtype),
        input_output_aliases={4: 0},
        compiler_params=_params(("parallel", "parallel")), name="sink_attn")(zr, zr, z, sink_vec, prev)


def _in_proj_layout(D):
    pool_w = POOL_GROUPS * (D // 8)
    widths = [("pool", pool_w), ("q_mla", MLA_HEADS * (MLA_NOPE + MLA_ROPE)), ("ckv", D // 4),
              ("k_rope", MLA_ROPE), ("q_win", WIN_HEADS * WIN_HEAD_DIM),
              ("k_win", WIN_KV_HEADS * WIN_HEAD_DIM), ("v_win", WIN_KV_HEADS * WIN_HEAD_DIM),
              ("q_diff", DIFF_HEADS * 2 * DIFF_HEAD_DIM), ("k_diff", DIFF_HEADS * 2 * DIFF_HEAD_DIM),
              ("v_diff", DIFF_HEADS * 2 * DIFF_HEAD_DIM), ("gates", N_BRANCHES * D)]
    out, off = {}, 0
    for name, w in widths:
        out[name] = (off, w)
        off += w
    return out


def _prep_in_proj(w_in, D):
    lay = _in_proj_layout(D)

    def cols(name):
        o, w = lay[name]
        return w_in[:, o:o + w]

    q_mla = cols("q_mla").reshape(D, MLA_HEADS, MLA_NOPE + MLA_ROPE) * (LOG2E * (MLA_NOPE + MLA_ROPE) ** -0.5)
    q_nope = q_mla[:, :, :MLA_NOPE].reshape(D, MLA_HEADS * MLA_NOPE)
    q_rope = jnp.pad(q_mla[:, :, MLA_NOPE:], ((0, 0), (0, 0), (0, LANES - MLA_ROPE))).reshape(D, MLA_HEADS * LANES)
    k_rope = jnp.pad(cols("k_rope"), ((0, 0), (0, LANES - MLA_ROPE)))
    plain = [cols("pool"), q_nope, cols("ckv"), cols("v_win"), cols("v_diff")]
    rot = [cols("q_win") * (LOG2E * WIN_HEAD_DIM ** -0.5), cols("q_diff") * (LOG2E * DIFF_HEAD_DIM ** -0.5),
           cols("k_win"), cols("k_diff"), q_rope, k_rope]
    n_plain = sum(x.shape[1] for x in plain)
    n_rot = sum(x.shape[1] for x in rot)
    total = n_plain + n_rot
    pad = (-total) % MXU_DIM
    w = jnp.concatenate(plain + rot + [jnp.zeros((D, pad), w_in.dtype)], axis=1).astype(BF16)
    return w, cols("gates").astype(BF16), n_plain, n_rot


def _rope_tables(n_lat, ctx):
    t = jnp.arange(n_lat)
    rows, cols_ = (t // GRID_W).astype(F32), (t % GRID_W).astype(F32)

    def table(half, pad):
        inv = ROPE_BASE ** (-jnp.arange(half, dtype=F32) / half)
        ar, ac = rows[:, None] * inv[None, :], cols_[:, None] * inv[None, :]
        cos = jnp.concatenate([jnp.cos(ar)] * 2 + [jnp.cos(ac)] * 2, axis=1)
        sin = jnp.concatenate([-jnp.sin(ar), jnp.sin(ar), -jnp.sin(ac), jnp.sin(ac)], axis=1)
        cos = jnp.pad(cos, ((0, ctx), (0, pad)), constant_values=1.0)
        sin = jnp.pad(sin, ((0, ctx), (0, pad)))
        return cos, sin

    return table(WIN_HEAD_DIM // 4, 0), table(MLA_ROPE // 4, LANES - MLA_ROPE)


def _apply_rope(zr, tabs, n_rot):
    B, T, _ = zr.shape
    nh = n_rot // LANES
    n64 = MLA_HEADS + 1
    x = zr.astype(F32).reshape(B, T, nh, LANES)

    def rot(x, cos, sin, half):
        xs = x.reshape(x.shape[:-1] + (LANES // (2 * half), 2, half))
        xs = jnp.flip(xs, axis=-2).reshape(x.shape)
        return x * cos[None, :, None, :] + xs * sin[None, :, None, :]

    (c128, s128), (c64, s64) = tabs
    a = rot(x[:, :, :nh - n64], c128, s128, WIN_HEAD_DIM // 4)
    b = rot(x[:, :, nh - n64:], c64, s64, MLA_ROPE // 4)
    return jnp.concatenate([a, b], axis=2).reshape(B, T, n_rot).astype(BF16)


def _pool_stencil(a, gd):
    B, n, _ = a.shape
    af = a.astype(F32)
    t = jnp.arange(n)
    outs = []
    for i, w in enumerate(POOL_WINDOWS):
        x = af[:, :, i * gd:(i + 1) * gd]
        xp = jnp.pad(x, ((0, 0), (w // 2, w // 2), (0, 0)))
        s = xp[:, 0:n]
        for dlt in range(1, w):
            s = s + xp[:, dlt:dlt + n]
        cnt = (jnp.clip(t - w // 2 + w, 0, n) - jnp.clip(t - w // 2, 0, n)).astype(F32)
        outs.append(s / cnt[None, :, None] - x)
    return jnp.concatenate(outs, axis=-1).astype(BF16)


def _layer(h, l, n_layers, P, mods, tabs, *, n_lat):
    B, T, D = h["h"].shape
    need_ctx = l < n_layers - 1
    rows = T if need_ctx else n_lat
    lam_init = 0.8 - 0.6 * math.exp(-0.3 * l)
    sh1, sc1, gt1, sh2, sc2, gt2 = mods

    if h["y"] is None:
        u = _resnorm(h["h"], P["norm1_g"][l], rows=T, n_lat=n_lat, shift=sh1, scale=sc1)
        hres = h["h"]
    else:
        hres, u = _resnorm(h["h"], P["norm1_g"][l], rows=T, n_lat=n_lat, y=h["y"], y_w=h["y_w"], gate=h["gate"],
                           shift=sh1, scale=sc1, emit_h=True)

    w_main, w_gate, n_plain, n_rot = _prep_in_proj(P["w_in"][l], D)
    z = _mm(u, w_main, rows=T, tn_cap=1536, name="in_proj")
    gates = _mm(u, w_gate, rows=rows, act="sigmoid", name="gate_proj")
    zr = _apply_rope(z[:, :, n_plain:n_plain + n_rot], tabs, n_rot)

    gd = D // 8
    pool_w = POOL_GROUPS * gd
    kvr = D // 4
    o_qn, o_ckv = pool_w, pool_w + MLA_HEADS * MLA_NOPE
    o_vw = o_ckv + kvr
    o_vd = o_vw + WIN_KV_HEADS * WIN_HEAD_DIM
    r_qd = WIN_HEADS * WIN_HEAD_DIM
    r_kw = r_qd + DIFF_HEADS * 2 * DIFF_HEAD_DIM
    r_kd = r_kw + WIN_KV_HEADS * WIN_HEAD_DIM
    r_qr = r_kd + DIFF_HEADS * 2 * DIFF_HEAD_DIM
    r_kr = r_qr + MLA_HEADS * LANES
    dd = 2 * DIFF_HEAD_DIM
    for off, blk in ((o_qn, LANES), (o_ckv, kvr), (o_vw, LANES), (o_vd, dd), (r_qd, dd), (r_kw, LANES),
                     (r_kd, dd), (r_qr, LANES), (r_kr, LANES)):
        assert off % blk == 0, (off, blk)
    cols = {"q_nope": o_qn // LANES, "q_rope": r_qr // LANES, "k_rope": r_kr // LANES,
            "q_diff": r_qd // dd, "k_diff": r_kd // dd, "v_diff": o_vd // dd,
            "q_win": 0, "k_win": r_kw // LANES, "v_win": o_vw // LANES}

    wkv = P["mla_w_kv_b"][l].reshape(kvr, MLA_HEADS, MLA_NOPE + MLA_V)
    wkv = jnp.concatenate([wkv[:, :, :MLA_NOPE].reshape(kvr, -1), wkv[:, :, MLA_NOPE:].reshape(kvr, -1)],
                          axis=1).astype(BF16)
    kv = _mm(z, wkv, rows=T, k_block=o_ckv // kvr, k_width=kvr, rms_g=P["mla_kv_norm_g"][l], name="kv_proj")

    a_pool = z[:, :, :pool_w]
    pooled = _pool_stencil(a_pool[:, :n_lat], gd)
    if need_ctx:
        pooled = jnp.concatenate([pooled, _pool_stencil(a_pool[:, n_lat:], gd)], axis=1)
    mixed = _group_mm(pooled, P["pool_w"][l].astype(BF16), rows=rows)

    mla_o = _mla_attn(z, zr, kv, n_lat=n_lat, need_ctx=need_ctx, cols=cols)
    lp = P["diff_lambda"][l].astype(F32)
    lam = jnp.exp(jnp.sum(lp[0] * lp[1])) - jnp.exp(jnp.sum(lp[2] * lp[3])) + lam_init
    lam_vec = jnp.full((1, dd), lam, F32)
    g_vec = (P["diff_subln_g"][l].astype(F32) * (1.0 - lam_init)).reshape(1, dd)
    diff_o = _diff_attn(z, zr, lam_vec, g_vec, n_lat=n_lat, need_ctx=need_ctx, cols=cols)
    sink_vec = jnp.repeat(P["win_sink"][l].astype(F32) * LOG2E, WIN_HEAD_DIM).reshape(1, WIN_HEADS * WIN_HEAD_DIM)
    win_o = _win_attn(z, zr, sink_vec, n_lat=n_lat, out_rows=rows, cols=cols)
    if need_ctx:
        win_o = _sink_attn(z, zr, sink_vec, win_o, n_lat=n_lat, cols=cols)

    w_pool_out = (P["pool_scale"][l][:, None] * P["pool_out"][l]).astype(BF16)
    merged = _merge([mixed, mla_o, win_o, diff_o],
                    [w_pool_out, P["mla_out"][l].astype(BF16), P["win_out"][l].astype(BF16),
                     P["diff_out"][l].astype(BF16)], gates, rows=rows)
    y1 = _mm(merged, P["w_out"][l].astype(BF16), rows=rows, name="out_proj")

    dense = l % 2 == 0
    h2, u2 = _resnorm(hres, P["norm2_g"][l], rows=rows, n_lat=n_lat, y=y1, gate=gt1, shift=sh2, scale=sc2,
                      emit_h=True, out_dtype=BF16 if dense else F32)
    j = l // 2
    if dense:
        y2 = _ffn(u2, P["ffn_w_gate"][j][None].astype(BF16), P["ffn_w_up"][j][None].astype(BF16),
                  P["ffn_w_down"][j][None].astype(BF16), rows=rows)
        return {"h": h2, "y": y2, "y_w": None, "gate": gt2}
    router = jnp.pad(P["moe_router"][j], ((0, 0), (0, LANES - N_EXPERTS)))
    logits = _mm(u2, router, rows=rows, out_dtype=F32, name="router")[:, :, :N_EXPERTS] + P["moe_router_b"][j]
    y2, top_w = _moe(u2.reshape(B * rows, D), logits.reshape(B * rows, N_EXPERTS), P["moe_w_gate"][j].astype(BF16),
                     P["moe_w_up"][j].astype(BF16), P["moe_w_down"][j].astype(BF16))
    return {"h": h2, "y": y2, "y_w": top_w.reshape(B, rows, TOP_K), "gate": gt2}


def kernel(x, c, ctx, c_ctx, w_mod, b_mod, norm1_g, norm2_g, w_in, pool_w, pool_scale, pool_out,
           mla_kv_norm_g, mla_w_kv_b, mla_out, win_sink, win_out, diff_lambda, diff_subln_g,
           diff_out, w_out, ffn_w_gate, ffn_w_up, ffn_w_down, moe_router, moe_router_b,
           moe_w_gate, moe_w_up, moe_w_down, final_norm_g):
    B, n_lat, D = x.shape
    n_ctx = ctx.shape[1]
    n_layers = w_in.shape[0]
    P = dict(norm1_g=norm1_g, norm2_g=norm2_g, w_in=w_in, pool_w=pool_w, pool_scale=pool_scale,
             pool_out=pool_out, mla_kv_norm_g=mla_kv_norm_g, mla_w_kv_b=mla_w_kv_b, mla_out=mla_out,
             win_sink=win_sink, win_out=win_out, diff_lambda=diff_lambda, diff_subln_g=diff_subln_g,
             diff_out=diff_out, w_out=w_out, ffn_w_gate=ffn_w_gate, ffn_w_up=ffn_w_up,
             ffn_w_down=ffn_w_down, moe_router=moe_router, moe_router_b=moe_router_b,
             moe_w_gate=moe_w_gate, moe_w_up=moe_w_up, moe_w_down=moe_w_down)
    tabs = _rope_tables(n_lat, n_ctx)

    cond = jnp.concatenate([c, c_ctx[None]], axis=0)
    cond = cond * (1.0 / (1.0 + jnp.exp(-cond)))
    n_pad = (-cond.shape[0]) % 8
    cond = jnp.pad(cond, ((0, n_pad), (0, 0)))[None]

    state = {"h": jnp.concatenate([x, ctx], axis=1), "y": None, "gate": None}
    for l in range(n_layers):
        mod = _mm(cond, w_mod[l], rows=cond.shape[1], out_dtype=F32, name="mod")[0, :B + 1] + b_mod[l]
        mods = [m.reshape(B + 1, 1, D) for m in jnp.split(mod, 6, axis=-1)]
        state = _layer(state, l, n_layers, P, mods, tabs, n_lat=n_lat)
    return _resnorm(state["h"], final_norm_g, rows=n_lat, n_lat=n_lat, y=state["y"], y_w=state["y_w"],
                    gate=state["gate"], out_dtype=F32)
```

```python
import functools
import math

import jax
import jax.numpy as jnp
from jax import lax
from jax.experimental import pallas as pl
from jax.experimental.pallas import tpu as pltpu

F32 = jnp.float32
BF16 = jnp.bfloat16

GRID_W = 64
ROPE_BASE = 10000.0
NORM_EPS = 1e-6
NEG_INF = -1e30
POOL_GROUPS = 4
POOL_WINDOWS = (2, 4, 8, 16)
MLA_HEADS = 8
MLA_NOPE = 128
MLA_ROPE = 64
MLA_V = 128
WIN_HEADS = 8
WIN_KV_HEADS = 2
WIN_HEAD_DIM = 128
WINDOW = 128
DIFF_HEADS = 4
DIFF_HEAD_DIM = 128
N_BRANCHES = 4
N_EXPERTS = 8
TOP_K = 2

LOG2E = math.log2(math.e)
LANES = 128
MXU_DIM = 256
VMEM_LIMIT = 56 * 1024 * 1024


def _tile(n, cap, mult):
    best = None
    for t in range(mult, min(n, cap) + 1, mult):
        if n % t == 0:
            best = t
    assert best is not None, (n, cap, mult)
    return best


def _params(sem):
    return pltpu.CompilerParams(dimension_semantics=sem, vmem_limit_bytes=VMEM_LIMIT)


def _resnorm_kernel(*refs, has_y, n_slots, has_mod, emit_h):
    it = iter(refs)
    h_ref = next(it)
    y_refs = [next(it) for _ in range(max(n_slots, 1))] if has_y else []
    yw_ref = next(it) if n_slots else None
    gate_ref = next(it) if has_y else None
    g_ref = next(it)
    sh_ref = next(it) if has_mod else None
    sc_ref = next(it) if has_mod else None
    hout_ref = next(it) if emit_h else None
    u_ref = next(it)
    h = h_ref[...]
    if has_y:
        if n_slots:
            y = sum(yw_ref[:, k:k + 1] * y_refs[k][...] for k in range(n_slots))
        else:
            y = y_refs[0][...].astype(F32)
        h = h + gate_ref[...] * y
    if emit_h:
        hout_ref[...] = h
    ms = jnp.mean(h * h, axis=-1, keepdims=True)
    v = h * lax.rsqrt(ms + NORM_EPS) * g_ref[...]
    if has_mod:
        v = v * (1.0 + sc_ref[...]) + sh_ref[...]
    u_ref[...] = v.astype(u_ref.dtype)


def _resnorm(h, norm_g, *, rows, n_lat, y=None, y_w=None, gate=None, shift=None, scale=None, emit_h=False,
             out_dtype=BF16):
    B, _, D = h.shape
    ctx = rows - n_lat
    tm = _tile(math.gcd(n_lat, ctx) if ctx else n_lat, 512, 16)
    n_lat_tiles = n_lat // tm
    has_y, has_mod = y is not None, shift is not None
    n_slots = 0 if y_w is None else y_w.shape[-1]
    row_spec = pl.BlockSpec((None, tm, D), lambda b, i: (b, i, 0))
    vec_spec = pl.BlockSpec((None, 1, D), lambda b, i: (jnp.where(i < n_lat_tiles, b, B), 0, 0))
    args, specs = [h], [row_spec]
    if n_slots:
        per_b = rows // tm
        args += [y] * n_slots + [y_w, gate]
        specs += [pl.BlockSpec((tm, D), lambda b, i, k=k: ((k * B + b) * per_b + i, 0)) for k in range(n_slots)]
        specs += [pl.BlockSpec((None, tm, n_slots), lambda b, i: (b, i, 0)), vec_spec]
    elif has_y:
        args += [y, gate]
        specs += [row_spec, vec_spec]
    args.append(norm_g.reshape(1, D))
    specs.append(pl.BlockSpec((1, D), lambda b, i: (0, 0)))
    if has_mod:
        args += [shift, scale]
        specs += [vec_spec, vec_spec]
    out_shape, out_specs = [], []
    if emit_h:
        out_shape.append(jax.ShapeDtypeStruct((B, rows, D), F32))
        out_specs.append(row_spec)
    out_shape.append(jax.ShapeDtypeStruct((B, rows, D), out_dtype))
    out_specs.append(row_spec)
    res = pl.pallas_call(
        functools.partial(_resnorm_kernel, has_y=has_y, n_slots=n_slots, has_mod=has_mod, emit_h=emit_h),
        grid=(B, rows // tm), in_specs=specs, out_specs=out_specs, out_shape=out_shape,
        compiler_params=_params(("parallel", "parallel")), name="resnorm")(*args)
    return res if emit_h else res[0]


def _mm_kernel(*refs, has_g, act):
    if has_g:
        a_ref, w_ref, g_ref, o_ref = refs
        af = a_ref[...].astype(F32)
        af = af * lax.rsqrt(jnp.mean(af * af, axis=-1, keepdims=True) + NORM_EPS) * g_ref[...]
        a = af.astype(BF16)
    else:
        a_ref, w_ref, o_ref = refs
        a = a_ref[...].astype(BF16)
    acc = jnp.dot(a, w_ref[...].astype(BF16), preferred_element_type=F32)
    if act == "sigmoid":
        acc = 1.0 / (1.0 + jnp.exp(-acc))
    o_ref[...] = acc.astype(o_ref.dtype)


def _mm(a, w, *, rows, k_block=0, k_width=None, rms_g=None, act=None, out_dtype=BF16,
        tm_cap=1100, tn_cap=1024, name="mm"):
    B = a.shape[0]
    K, N = w.shape
    if k_width is None:
        assert a.shape[2] == K
    tm = _tile(rows, tm_cap, 8 if rows < 16 else 16)
    tn = _tile(N, tn_cap, MXU_DIM if N % MXU_DIM == 0 else LANES)
    args = [a, w]
    specs = [pl.BlockSpec((None, tm, K), lambda b, i, j: (b, i, k_block)),
             pl.BlockSpec((K, tn), lambda b, i, j: (0, j))]
    if rms_g is not None:
        args.append(rms_g.reshape(1, K).astype(F32))
        specs.append(pl.BlockSpec((1, K), lambda b, i, j: (0, 0)))
    return pl.pallas_call(
        functools.partial(_mm_kernel, has_g=rms_g is not None, act=act),
        grid=(B, rows // tm, N // tn), in_specs=specs,
        out_specs=pl.BlockSpec((None, tm, tn), lambda b, i, j: (b, i, j)),
        out_shape=jax.ShapeDtypeStruct((B, rows, N), out_dtype),
        compiler_params=_params(("parallel", "parallel", "arbitrary")), name=name)(*args)


def _mm_rope_kernel(a_ref, w_ref, cos_ref, sin_hi_ref, sin_lo_ref, o_ref, *, half):
    acc = jnp.dot(a_ref[...], w_ref[...], preferred_element_type=F32)
    cos, sin_hi, sin_lo = cos_ref[...], sin_hi_ref[...], sin_lo_ref[...]
    for hd in range(acc.shape[1] // LANES):
        x = acc[:, hd * LANES:(hd + 1) * LANES]
        y = x * cos + pltpu.roll(x, half, 1) * sin_hi + pltpu.roll(x, LANES - half, 1) * sin_lo
        o_ref[:, hd * LANES:(hd + 1) * LANES] = y.astype(o_ref.dtype)


def _mm_rope(a, w, tables, *, half, rows, tn_cap, name):
    B, _, K = a.shape
    N = w.shape[1]
    tm = _tile(rows, 1100, 16)
    tn = _tile(N, tn_cap, LANES)
    tab_spec = pl.BlockSpec((tm, LANES), lambda b, i, j: (i, 0))
    return pl.pallas_call(
        functools.partial(_mm_rope_kernel, half=half), grid=(B, rows // tm, N // tn),
        in_specs=[pl.BlockSpec((None, tm, K), lambda b, i, j: (b, i, 0)),
                  pl.BlockSpec((K, tn), lambda b, i, j: (0, j)), tab_spec, tab_spec, tab_spec],
        out_specs=pl.BlockSpec((None, tm, tn), lambda b, i, j: (b, i, j)),
        out_shape=jax.ShapeDtypeStruct((B, rows, N), BF16),
        compiler_params=_params(("parallel", "parallel", "arbitrary")), name=name)(a, w, *tables)


POOL_HALO = 16


def _pool_kernel(prev_ref, x_ref, next_ref, w_ref, o_ref, *, n_lat, n_rows):
    tm = x_ref.shape[0]
    i = pl.program_id(1)
    half_w = jnp.left_shift(1, pl.program_id(2))
    xh = jnp.concatenate([prev_ref[...], x_ref[...], next_ref[...]], axis=0)
    shape = (tm, tm + 2 * POOL_HALO)
    row0 = i * tm
    r = lax.broadcasted_iota(jnp.int32, shape, 0) + row0
    src = lax.broadcasted_iota(jnp.int32, shape, 1) + (row0 - POOL_HALO)
    latent = row0 < n_lat
    lo = jnp.maximum(r - half_w, jnp.where(latent, 0, n_lat))
    hi = jnp.minimum(r + half_w, jnp.where(latent, n_lat, n_rows))
    band = jnp.where((src >= lo) & (src < hi), 1.0 / (hi - lo).astype(F32), 0.0) - jnp.where(src == r, 1.0, 0.0)
    pooled = jnp.dot(band.astype(BF16), xh, preferred_element_type=F32)
    o_ref[...] = jnp.dot(pooled.astype(BF16), w_ref[...], preferred_element_type=F32).astype(o_ref.dtype)


def _pool_mix(z, w, *, rows, n_lat):
    B, T, _ = z.shape
    G, gd, _ = w.shape
    assert tuple(2 << g for g in range(G)) == POOL_WINDOWS and max(POOL_WINDOWS) // 2 <= POOL_HALO
    ctx = rows - n_lat
    tm = _tile(math.gcd(n_lat, ctx) if ctx else n_lat, 512, POOL_HALO)
    per = tm // POOL_HALO
    last = T // POOL_HALO - 1
    return pl.pallas_call(
        functools.partial(_pool_kernel, n_lat=n_lat, n_rows=rows), grid=(B, rows // tm, G),
        in_specs=[pl.BlockSpec((None, POOL_HALO, gd), lambda b, i, g: (b, jnp.maximum(i * per - 1, 0), g)),
                  pl.BlockSpec((None, tm, gd), lambda b, i, g: (b, i, g)),
                  pl.BlockSpec((None, POOL_HALO, gd), lambda b, i, g: (b, jnp.minimum((i + 1) * per, last), g)),
                  pl.BlockSpec((None, gd, gd), lambda b, i, g: (g, 0, 0))],
        out_specs=pl.BlockSpec((None, tm, gd), lambda b, i, g: (b, i, g)),
        out_shape=jax.ShapeDtypeStruct((B, rows, G * gd), BF16),
        compiler_params=_params(("parallel", "parallel", "arbitrary")), name="pool_mix")(z, z, z, w)


def _merge_kernel(a0, a1, a2, a3, w0, w1, w2, w3, g0, g1, g2, g3, o_ref):
    acc = None
    for a_ref, w_ref, g_ref in ((a0, w0, g0), (a1, w1, g1), (a2, w2, g2), (a3, w3, g3)):
        y = jnp.dot(a_ref[...], w_ref[...], preferred_element_type=F32) * g_ref[...].astype(F32)
        acc = y if acc is None else acc + y
    o_ref[...] = acc.astype(o_ref.dtype)


def _merge(branches, weights, gates, *, rows):
    B = branches[0].shape[0]
    N = weights[0].shape[1]
    tm = _tile(rows, 1100, 16)
    tn = _tile(N, 512, LANES)
    nj = N // tn
    a_specs = [pl.BlockSpec((None, tm, w.shape[0]), lambda b, i, j: (b, i, 0)) for w in weights]
    w_specs = [pl.BlockSpec((w.shape[0], tn), lambda b, i, j: (0, j)) for w in weights]
    g_specs = [pl.BlockSpec((None, tm, tn), lambda b, i, j, br=br: (b, i, br * nj + j))
               for br in range(N_BRANCHES)]
    return pl.pallas_call(
        _merge_kernel, grid=(B, rows // tm, nj),
        in_specs=a_specs + w_specs + g_specs,
        out_specs=pl.BlockSpec((None, tm, tn), lambda b, i, j: (b, i, j)),
        out_shape=jax.ShapeDtypeStruct((B, rows, N), BF16),
        compiler_params=_params(("parallel", "parallel", "arbitrary")),
        name="merge")(*branches, *weights, gates, gates, gates, gates)


def _ffn_kernel(*refs, has_rs, n_steps):
    if has_rs:
        a_ref, wg_ref, wu_ref, wd_ref, rs_ref, o_ref, acc_ref = refs
    else:
        a_ref, wg_ref, wu_ref, wd_ref, o_ref, acc_ref = refs
    k = pl.program_id(2)

    @pl.when(k == 0)
    def _():
        acc_ref[...] = jnp.zeros_like(acc_ref)

    a = a_ref[...]
    g = jnp.dot(a, wg_ref[...], preferred_element_type=F32)
    u = jnp.dot(a, wu_ref[...], preferred_element_type=F32)
    hid = g * (1.0 / (1.0 + jnp.exp(-g))) * u
    if has_rs:
        hid = hid * rs_ref[...]
    acc_ref[...] += jnp.dot(hid.astype(BF16), wd_ref[...], preferred_element_type=F32)

    @pl.when(k == n_steps - 1)
    def _():
        o_ref[...] = acc_ref[...].astype(o_ref.dtype)


def _ffn(a, wg, wu, wd, *, rows, row_scale=None):
    B, _, D = a.shape
    E, _, F = wg.shape
    tm = _tile(rows, 1100, 16)
    tf = _tile(F, 512, LANES)
    nf = F // tf
    n_steps = E * nf
    args = [a, wg, wu, wd]
    specs = [pl.BlockSpec((None, tm, D), lambda b, i, k: (b, i, 0)),
             pl.BlockSpec((None, D, tf), lambda b, i, k: (k // nf, 0, k % nf)),
             pl.BlockSpec((None, D, tf), lambda b, i, k: (k // nf, 0, k % nf)),
             pl.BlockSpec((None, tf, D), lambda b, i, k: (k // nf, k % nf, 0))]
    if row_scale is not None:
        args.append(row_scale)
        specs.append(pl.BlockSpec((None, None, tm, 1), lambda b, i, k: (b, k // nf, i, 0)))
    return pl.pallas_call(
        functools.partial(_ffn_kernel, has_rs=row_scale is not None, n_steps=n_steps),
        grid=(B, rows // tm, n_steps), in_specs=specs,
        out_specs=pl.BlockSpec((None, tm, D), lambda b, i, k: (b, i, 0)),
        out_shape=jax.ShapeDtypeStruct((B, rows, D), BF16),
        scratch_shapes=[pltpu.VMEM((tm, D), F32)],
        compiler_params=_params(("parallel", "parallel", "arbitrary")), name="ffn")(*args)


def _moe_kernel(te_ref, nu_ref, x_hbm, ord_hbm, wg_ref, wu_ref, wd_ref, y_hbm,
                gbuf, xb, acc, idx, sem_g, sem_s, sem_i, *, tm, nf, n_tok):
    del te_ref
    t = pl.program_id(0)
    f = pl.program_id(1)
    n_used = nu_ref[0]
    slot = t % 2

    def idx_copy(tile, s):
        return pltpu.make_async_copy(ord_hbm.at[pl.ds(tile * tm, tm)], idx.at[s], sem_i)

    def gather_copy(tok, r):
        return pltpu.make_async_copy(x_hbm.at[pl.ds(tok, 1)], gbuf.at[pl.ds(r, 1)], sem_g)

    def scatter_copy(s, r, dst):
        return pltpu.make_async_copy(acc.at[s, pl.ds(r, 1)], y_hbm.at[pl.ds(dst, 1)], sem_s.at[s])

    def issue_gather(s):
        def body(r, c):
            gather_copy(jnp.maximum(idx[s, r], 0) // TOP_K, r).start()
            return c
        lax.fori_loop(0, tm, body, 0, unroll=DMA_UNROLL)

    def wait_gather():
        pltpu.make_async_copy(x_hbm.at[pl.ds(0, tm)], gbuf, sem_g).wait()

    def issue_scatter(s):
        def body(r, c):
            a = idx[s, r]
            dst = jnp.where(a >= 0, (a % TOP_K) * n_tok + a // TOP_K, TOP_K * n_tok + r)
            scatter_copy(s, r, dst).start()
            return c
        lax.fori_loop(0, tm, body, 0, unroll=DMA_UNROLL)

    def wait_scatter(s):
        pltpu.make_async_copy(acc.at[s], y_hbm.at[pl.ds(0, tm)], sem_s.at[s]).wait()

    @pl.when(t < n_used)
    def _():
        @pl.when(f == 0)
        def _():
            @pl.when(t == 0)
            def _():
                first = idx_copy(0, 0)
                first.start()
                first.wait()
                issue_gather(0)

            wait_gather()
            xb[...] = gbuf[...].astype(BF16)

            @pl.when(t + 1 < n_used)
            def _():
                idx_copy(t + 1, 1 - slot).start()

        @pl.when((f == 1) & (t + 1 < n_used))
        def _():
            idx_copy(t + 1, 1 - slot).wait()
            issue_gather(1 - slot)

        a = xb[...]
        g = jnp.dot(a, wg_ref[...], preferred_element_type=F32)
        u = jnp.dot(a, wu_ref[...], preferred_element_type=F32)
        hid = (g * (1.0 / (1.0 + jnp.exp(-g))) * u).astype(BF16)
        part = jnp.dot(hid, wd_ref[...], preferred_element_type=F32)

        @pl.when(f == 0)
        def _():
            acc[slot] = part

        @pl.when(f > 0)
        def _():
            acc[slot] += part

        @pl.when(f == nf - 1)
        def _():
            issue_scatter(slot)

            @pl.when(t > 0)
            def _():
                wait_scatter(1 - slot)

            @pl.when(t == n_used - 1)
            def _():
                wait_scatter(slot)


MOE_TILE_ROWS = 1024
DMA_UNROLL = 8


def _moe_route(logits, tm):
    n_tok = logits.shape[0]
    top_v, top_i = lax.top_k(logits, TOP_K)
    top_w = jax.nn.softmax(top_v, axis=-1)
    e_flat = top_i.reshape(-1).astype(jnp.int32)
    n_asg = e_flat.shape[0]
    cnt = jnp.sum(jax.nn.one_hot(e_flat, N_EXPERTS, dtype=jnp.int32), axis=0)
    pad = (-cnt) % tm
    big = 2 * N_EXPERTS
    j = jnp.arange(tm, dtype=jnp.int32)[None, :]
    e = jnp.arange(N_EXPERTS, dtype=jnp.int32)[:, None]
    pad_keys = jnp.where(j < pad[:, None], 2 * e + 1, big).reshape(-1)
    keys = jnp.concatenate([2 * e_flat, pad_keys])
    vals = jnp.concatenate([jnp.arange(n_asg, dtype=jnp.int32), jnp.full((N_EXPERTS * tm,), -1, jnp.int32)])
    keys, order = lax.sort_key_val(keys, vals)
    n_tiles = keys.shape[0] // tm
    tile_key = keys.reshape(n_tiles, tm)[:, 0]
    used = tile_key < big
    n_used = jnp.sum(used).astype(jnp.int32)
    tile_e = jnp.where(used, tile_key // 2, 0)
    tile_e = jnp.where(used, tile_e, tile_e[jnp.maximum(n_used - 1, 0)])
    return top_w, order, tile_e.astype(jnp.int32), n_used.reshape(1)


def _moe(x, logits, wg, wu, wd):
    n_tok, D = x.shape
    E, _, F = wg.shape
    tm = _tile(n_tok, MOE_TILE_ROWS, 16)
    tf = _tile(F, 512, LANES)
    nf = F // tf
    assert nf >= 2
    top_w, order, tile_e, n_used = _moe_route(logits, tm)
    n_tiles = order.shape[0] // tm

    def w_in_map(t, f, te, nu):
        return (te[t], 0, jnp.where(t < nu[0], f, nf - 1))

    def w_out_map(t, f, te, nu):
        return (te[t], jnp.where(t < nu[0], f, nf - 1), 0)

    grid_spec = pltpu.PrefetchScalarGridSpec(
        num_scalar_prefetch=2, grid=(n_tiles, nf),
        in_specs=[pl.BlockSpec(memory_space=pl.ANY), pl.BlockSpec(memory_space=pl.ANY),
                  pl.BlockSpec((None, D, tf), w_in_map), pl.BlockSpec((None, D, tf), w_in_map),
                  pl.BlockSpec((None, tf, D), w_out_map)],
        out_specs=pl.BlockSpec(memory_space=pl.ANY),
        scratch_shapes=[pltpu.VMEM((tm, D), F32), pltpu.VMEM((tm, D), BF16), pltpu.VMEM((2, tm, D), F32),
                        pltpu.SMEM((2, tm), jnp.int32), pltpu.SemaphoreType.DMA(()),
                        pltpu.SemaphoreType.DMA((2,)), pltpu.SemaphoreType.DMA(())])
    y = pl.pallas_call(
        functools.partial(_moe_kernel, tm=tm, nf=nf, n_tok=n_tok), grid_spec=grid_spec,
        out_shape=jax.ShapeDtypeStruct((TOP_K * n_tok + tm, D), F32),
        compiler_params=_params(("arbitrary", "arbitrary")), name="moe")(
            tile_e, n_used, x, order, wg, wu, wd)
    return y, top_w


ATTN_CHAIN_ROWS = 256


def _softmax_rows(s):
    p = jnp.exp2(s - jnp.max(s, axis=-1, keepdims=True))
    return p.astype(BF16), jnp.sum(p, axis=-1, keepdims=True)


def _chains(tq):
    r = min(tq, ATTN_CHAIN_ROWS)
    return [slice(a * r, (a + 1) * r) for a in range(tq // r)]


def _qk(q, k):
    return lax.dot_general(q, k, (((1,), (1,)), ((), ())), preferred_element_type=F32)


def _mla_kernel(qn_ref, qr_ref, kn_ref, kr_ref, v_ref, *rest):
    o_ref = rest[-1]
    k = jnp.concatenate([kn_ref[...], kr_ref[...]], axis=-1)
    v = v_ref[...]
    for rows in _chains(qn_ref.shape[0]):
        q = jnp.concatenate([qn_ref[rows, :], qr_ref[rows, :]], axis=-1)
        p, l = _softmax_rows(_qk(q, k))
        o_ref[rows, :] = (jnp.dot(p, v, preferred_element_type=F32) / l).astype(o_ref.dtype)


def _attn_calls(kernel, make_specs, args, out_width, vec_args, *, B, H, T, n_lat, need_ctx, tq_cap, name):
    ctx = T - n_lat
    out_rows = T if need_ctx else n_lat

    def call(tq, q_blk0, nq, k_rows, k_blk, prev):
        specs = make_specs(tq, q_blk0, k_rows, k_blk)
        specs += [pl.BlockSpec(v.shape, lambda b, h, i: (0, 0)) for v in vec_args]
        ins = list(args) + list(vec_args)
        aliases = {}
        if prev is not None:
            specs.append(pl.BlockSpec(memory_space=pl.ANY))
            aliases = {len(ins): 0}
            ins.append(prev)
        return pl.pallas_call(
            kernel, grid=(B, H, nq), in_specs=specs,
            out_specs=pl.BlockSpec((None, tq, out_width), lambda b, h, i: (b, q_blk0 + i, h)),
            out_shape=jax.ShapeDtypeStruct((B, out_rows, H * out_width), BF16),
            input_output_aliases=aliases,
            compiler_params=_params(("parallel", "parallel", "arbitrary")), name=name)(*ins)

    tq = _tile(n_lat, tq_cap, ATTN_CHAIN_ROWS)
    out = call(tq, 0, n_lat // tq, T, 0, None)
    if need_ctx:
        out = call(ctx, n_lat // ctx, 1, ctx, n_lat // ctx, out)
    return out


def _mla_attn(z, zr, kv, *, n_lat, need_ctx, cols):
    B, T, _ = z.shape
    H = MLA_HEADS
    qn0, qr0, kr0 = cols["q_nope"], cols["q_rope"], cols["k_rope"]

    def specs(tq, q_blk0, k_rows, k_blk):
        return [pl.BlockSpec((None, tq, LANES), lambda b, h, i: (b, q_blk0 + i, qn0 + h)),
                pl.BlockSpec((None, tq, LANES), lambda b, h, i: (b, q_blk0 + i, qr0 + h)),
                pl.BlockSpec((None, k_rows, LANES), lambda b, h, i: (b, k_blk, h)),
                pl.BlockSpec((None, k_rows, LANES), lambda b, h, i: (b, k_blk, kr0)),
                pl.BlockSpec((None, k_rows, LANES), lambda b, h, i: (b, k_blk, H + h))]

    return _attn_calls(_mla_kernel, specs, (z, zr, kv, zr, kv), MLA_V, (), B=B, H=H, T=T, n_lat=n_lat,
                       need_ctx=need_ctx, tq_cap=1024, name="mla_attn")


def _diff_kernel(q_ref, k_ref, v_ref, lam_ref, g_ref, *rest):
    o_ref = rest[-1]
    d = DIFF_HEAD_DIM
    k = k_ref[...]
    k1, k2 = k[:, :d], k[:, d:]
    v = v_ref[...]
    for rows in _chains(q_ref.shape[0]):
        q = q_ref[rows, :]
        p1, l1 = _softmax_rows(_qk(q[:, :d], k1))
        p2, l2 = _softmax_rows(_qk(q[:, d:], k2))
        o = (jnp.dot(p1, v, preferred_element_type=F32) / l1
             - lam_ref[...] * (jnp.dot(p2, v, preferred_element_type=F32) / l2))
        o = o * lax.rsqrt(jnp.mean(o * o, axis=-1, keepdims=True) + NORM_EPS) * g_ref[...]
        o_ref[rows, :] = o.astype(o_ref.dtype)


def _diff_attn(z, zr, lam_vec, g_vec, *, n_lat, need_ctx, cols):
    B, T, _ = z.shape
    dd = 2 * DIFF_HEAD_DIM
    q0, k0, v0 = cols["q_diff"], cols["k_diff"], cols["v_diff"]

    def specs(tq, q_blk0, k_rows, k_blk):
        return [pl.BlockSpec((None, tq, dd), lambda b, h, i: (b, q_blk0 + i, q0 + h)),
                pl.BlockSpec((None, k_rows, dd), lambda b, h, i: (b, k_blk, k0 + h)),
                pl.BlockSpec((None, k_rows, dd), lambda b, h, i: (b, k_blk, v0 + h))]

    return _attn_calls(_diff_kernel, specs, (zr, zr, z), dd, (lam_vec, g_vec), B=B, H=DIFF_HEADS, T=T,
                       n_lat=n_lat, need_ctx=need_ctx, tq_cap=512, name="diff_attn")


def _win_kernel(q_ref, kp_ref, kc_ref, kn_ref, kx_ref, vp_ref, vc_ref, vn_ref, vx_ref, sink_ref,
                o_ref, *, n_lat):
    d = WIN_HEAD_DIM
    tq = q_ref.shape[0]
    i = pl.program_id(2)
    k = jnp.concatenate([kp_ref[...], kc_ref[...], kn_ref[...], kx_ref[...]], axis=0)
    v = jnp.concatenate([vp_ref[...], vc_ref[...], vn_ref[...], vx_ref[...]], axis=0)
    n_loc = tq + 2 * WINDOW
    nk = k.shape[0]
    r = lax.broadcasted_iota(jnp.int32, (tq, nk), 0)
    c = lax.broadcasted_iota(jnp.int32, (tq, nk), 1)
    kpos = i * tq - WINDOW + c
    rel = c - WINDOW - r
    valid = (c >= n_loc) | ((jnp.abs(rel) <= WINDOW) & (kpos >= 0) & (kpos < n_lat))
    outs = []
    for g in range(WIN_HEADS // WIN_KV_HEADS):
        q = q_ref[:, g * d:(g + 1) * d]
        s = lax.dot_general(q, k, (((1,), (1,)), ((), ())), preferred_element_type=F32)
        s = jnp.where(valid, s, NEG_INF)
        sink = sink_ref[:, g * d:g * d + 1]
        m = jnp.maximum(jnp.max(s, axis=-1, keepdims=True), sink)
        p = jnp.exp2(s - m)
        l = jnp.sum(p, axis=-1, keepdims=True) + jnp.exp2(sink - m)
        o = jnp.dot(p.astype(BF16), v, preferred_element_type=F32)
        outs.append(o / l)
    o_ref[...] = jnp.concatenate(outs, axis=-1).astype(o_ref.dtype)


def _win_attn(z, zr, sink_vec, *, n_lat, out_rows, cols):
    B, T, _ = z.shape
    d = WIN_HEAD_DIM
    grp = WIN_HEADS // WIN_KV_HEADS
    ctx = T - n_lat
    tq = _tile(n_lat, 512, WINDOW)
    per = tq // WINDOW
    q0, k0, v0 = cols["q_win"], cols["k_win"], cols["v_win"]

    def halo(col0):
        return [pl.BlockSpec((None, WINDOW, d), lambda b, h, i: (b, jnp.maximum(i * per - 1, 0), col0 + h)),
                pl.BlockSpec((None, tq, d), lambda b, h, i: (b, i, col0 + h)),
                pl.BlockSpec((None, WINDOW, d), lambda b, h, i: (b, (i + 1) * per, col0 + h)),
                pl.BlockSpec((None, ctx, d), lambda b, h, i: (b, n_lat // ctx, col0 + h))]

    return pl.pallas_call(
        functools.partial(_win_kernel, n_lat=n_lat), grid=(B, WIN_KV_HEADS, n_lat // tq),
        in_specs=[pl.BlockSpec((None, tq, grp * d), lambda b, h, i: (b, i, q0 + h))]
        + halo(k0) + halo(v0) + [pl.BlockSpec((1, grp * d), lambda b, h, i: (0, h))],
        out_specs=pl.BlockSpec((None, tq, grp * d), lambda b, h, i: (b, i, h)),
        out_shape=jax.ShapeDtypeStruct((B, out_rows, WIN_HEADS * d), BF16),
        compiler_params=_params(("parallel", "parallel", "arbitrary")),
        name="win_attn")(zr, zr, zr, zr, zr, z, z, z, z, sink_vec)


def _sink_kernel(q_ref, k_ref, v_ref, sink_ref, prev_ref, o_ref):
    del prev_ref
    d = WIN_HEAD_DIM
    k = k_ref[...]
    v = v_ref[...]
    outs = []
    for g in range(WIN_HEADS // WIN_KV_HEADS):
        q = q_ref[:, g * d:(g + 1) * d]
        s = lax.dot_general(q, k, (((1,), (1,)), ((), ())), preferred_element_type=F32)
        sink = sink_ref[:, g * d:g * d + 1]
        m = jnp.maximum(jnp.max(s, axis=-1, keepdims=True), sink)
        p = jnp.exp2(s - m)
        l = jnp.sum(p, axis=-1, keepdims=True) + jnp.exp2(sink - m)
        outs.append(jnp.dot(p.astype(BF16), v, preferred_element_type=F32) / l)
    o_ref[...] = jnp.concatenate(outs, axis=-1).astype(o_ref.dtype)


def _sink_attn(z, zr, sink_vec, prev, *, n_lat, cols):
    B, T, _ = z.shape
    d = WIN_HEAD_DIM
    grp = WIN_HEADS // WIN_KV_HEADS
    ctx = T - n_lat
    blk = n_lat // ctx
    q0, k0, v0 = cols["q_win"], cols["k_win"], cols["v_win"]
    return pl.pallas_call(
        _sink_kernel, grid=(B, WIN_KV_HEADS),
        in_specs=[pl.BlockSpec((None, ctx, grp * d), lambda b, h: (b, blk, q0 + h)),
                  pl.BlockSpec((None, ctx, d), lambda b, h: (b, blk, k0 + h)),
                  pl.BlockSpec((None, ctx, d), lambda b, h: (b, blk, v0 + h)),
                  pl.BlockSpec((1, grp * d), lambda b, h: (0, h)),
                  pl.BlockSpec(memory_space=pl.ANY)],
        out_specs=pl.BlockSpec((None, ctx, grp * d), lambda b, h: (b, blk, h)),
        out_shape=jax.ShapeDtypeStruct(prev.shape, prev.dtype),
        input_output_aliases={4: 0},
        compiler_params=_params(("parallel", "parallel")), name="sink_attn")(zr, zr, z, sink_vec, prev)


def _in_proj_layout(D):
    pool_w = POOL_GROUPS * (D // 8)
    widths = [("pool", pool_w), ("q_mla", MLA_HEADS * (MLA_NOPE + MLA_ROPE)), ("ckv", D // 4),
              ("k_rope", MLA_ROPE), ("q_win", WIN_HEADS * WIN_HEAD_DIM),
              ("k_win", WIN_KV_HEADS * WIN_HEAD_DIM), ("v_win", WIN_KV_HEADS * WIN_HEAD_DIM),
              ("q_diff", DIFF_HEADS * 2 * DIFF_HEAD_DIM), ("k_diff", DIFF_HEADS * 2 * DIFF_HEAD_DIM),
              ("v_diff", DIFF_HEADS * 2 * DIFF_HEAD_DIM), ("gates", N_BRANCHES * D)]
    out, off = {}, 0
    for name, w in widths:
        out[name] = (off, w)
        off += w
    return out


def _prep_in_proj(w_in, D):
    lay = _in_proj_layout(D)

    def cols(name):
        o, w = lay[name]
        return w_in[:, o:o + w]

    q_mla = cols("q_mla").reshape(D, MLA_HEADS, MLA_NOPE + MLA_ROPE) * (LOG2E * (MLA_NOPE + MLA_ROPE) ** -0.5)
    q_nope = q_mla[:, :, :MLA_NOPE].reshape(D, MLA_HEADS * MLA_NOPE)
    q_rope = jnp.pad(q_mla[:, :, MLA_NOPE:], ((0, 0), (0, 0), (0, LANES - MLA_ROPE))).reshape(D, MLA_HEADS * LANES)
    k_rope = jnp.pad(cols("k_rope"), ((0, 0), (0, LANES - MLA_ROPE)))
    plain = [cols("pool"), q_nope, cols("ckv"), cols("v_win"), cols("v_diff")]
    rot = [cols("q_win") * (LOG2E * WIN_HEAD_DIM ** -0.5), cols("q_diff") * (LOG2E * DIFF_HEAD_DIM ** -0.5),
           cols("k_win"), cols("k_diff")]
    rot_mla = [q_rope, k_rope]
    n_mla = sum(x.shape[1] for x in rot_mla)
    rot_mla.append(jnp.zeros((D, (-n_mla) % MXU_DIM), w_in.dtype))
    cat = lambda parts: jnp.concatenate(parts, axis=1).astype(BF16)
    return cat(plain), cat(rot), cat(rot_mla), cols("gates").astype(BF16)


def _rope_tables(n_lat, ctx):
    t = jnp.arange(n_lat)
    rows, cols_ = (t // GRID_W).astype(F32), (t % GRID_W).astype(F32)

    def table(half, pad):
        inv = ROPE_BASE ** (-jnp.arange(half, dtype=F32) / half)
        ar, ac = rows[:, None] * inv[None, :], cols_[:, None] * inv[None, :]
        zero = jnp.zeros_like(ar)
        cos = jnp.concatenate([jnp.cos(ar)] * 2 + [jnp.cos(ac)] * 2, axis=1)
        sin_hi = jnp.concatenate([zero, jnp.sin(ar), zero, jnp.sin(ac)], axis=1)
        sin_lo = jnp.concatenate([-jnp.sin(ar), zero, -jnp.sin(ac), zero], axis=1)
        cos = jnp.pad(cos, ((0, ctx), (0, pad)), constant_values=1.0)
        return cos, jnp.pad(sin_hi, ((0, ctx), (0, pad))), jnp.pad(sin_lo, ((0, ctx), (0, pad)))

    return table(WIN_HEAD_DIM // 4, 0), table(MLA_ROPE // 4, LANES - MLA_ROPE)


def _layer(h, l, n_layers, P, mods, tabs, *, n_lat):
    B, T, D = h["h"].shape
    need_ctx = l < n_layers - 1
    rows = T if need_ctx else n_lat
    lam_init = 0.8 - 0.6 * math.exp(-0.3 * l)
    sh1, sc1, gt1, sh2, sc2, gt2 = mods

    if h["y"] is None:
        u = _resnorm(h["h"], P["norm1_g"][l], rows=T, n_lat=n_lat, shift=sh1, scale=sc1)
        hres = h["h"]
    else:
        hres, u = _resnorm(h["h"], P["norm1_g"][l], rows=T, n_lat=n_lat, y=h["y"], y_w=h["y_w"], gate=h["gate"],
                           shift=sh1, scale=sc1, emit_h=True)

    w_plain, w_rot, w_rot_mla, w_gate = _prep_in_proj(P["w_in"][l], D)
    z = _mm(u, w_plain, rows=T, tn_cap=1536, name="in_proj")
    zr = _mm_rope(u, w_rot, tabs[0], half=WIN_HEAD_DIM // 4, rows=T, tn_cap=1792, name="in_proj_rot")
    zm = _mm_rope(u, w_rot_mla, tabs[1], half=MLA_ROPE // 4, rows=T, tn_cap=1792, name="in_proj_rot_mla")
    gates = _mm(u, w_gate, rows=rows, act="sigmoid", name="gate_proj")

    gd = D // 8
    pool_w = POOL_GROUPS * gd
    kvr = D // 4
    o_qn, o_ckv = pool_w, pool_w + MLA_HEADS * MLA_NOPE
    o_vw = o_ckv + kvr
    o_vd = o_vw + WIN_KV_HEADS * WIN_HEAD_DIM
    r_qd = WIN_HEADS * WIN_HEAD_DIM
    r_kw = r_qd + DIFF_HEADS * 2 * DIFF_HEAD_DIM
    r_kd = r_kw + WIN_KV_HEADS * WIN_HEAD_DIM
    dd = 2 * DIFF_HEAD_DIM
    for off, blk in ((o_qn, LANES), (o_ckv, kvr), (o_vw, LANES), (o_vd, dd), (r_qd, dd), (r_kw, LANES), (r_kd, dd)):
        assert off % blk == 0, (off, blk)
    cols = {"q_nope": o_qn // LANES, "q_rope": 0, "k_rope": MLA_HEADS,
            "q_diff": r_qd // dd, "k_diff": r_kd // dd, "v_diff": o_vd // dd,
            "q_win": 0, "k_win": r_kw // LANES, "v_win": o_vw // LANES}

    wkv = P["mla_w_kv_b"][l].reshape(kvr, MLA_HEADS, MLA_NOPE + MLA_V)
    wkv = jnp.concatenate([wkv[:, :, :MLA_NOPE].reshape(kvr, -1), wkv[:, :, MLA_NOPE:].reshape(kvr, -1)],
                          axis=1).astype(BF16)
    kv = _mm(z, wkv, rows=T, k_block=o_ckv // kvr, k_width=kvr, rms_g=P["mla_kv_norm_g"][l], name="kv_proj")

    mixed = _pool_mix(z, P["pool_w"][l].astype(BF16), rows=rows, n_lat=n_lat)

    mla_o = _mla_attn(z, zm, kv, n_lat=n_lat, need_ctx=need_ctx, cols=cols)
    lp = P["diff_lambda"][l].astype(F32)
    lam = jnp.exp(jnp.sum(lp[0] * lp[1])) - jnp.exp(jnp.sum(lp[2] * lp[3])) + lam_init
    lam_vec = jnp.full((1, dd), lam, F32)
    g_vec = (P["diff_subln_g"][l].astype(F32) * (1.0 - lam_init)).reshape(1, dd)
    diff_o = _diff_attn(z, zr, lam_vec, g_vec, n_lat=n_lat, need_ctx=need_ctx, cols=cols)
    sink_vec = jnp.repeat(P["win_sink"][l].astype(F32) * LOG2E, WIN_HEAD_DIM).reshape(1, WIN_HEADS * WIN_HEAD_DIM)
    win_o = _win_attn(z, zr, sink_vec, n_lat=n_lat, out_rows=rows, cols=cols)
    if need_ctx:
        win_o = _sink_attn(z, zr, sink_vec, win_o, n_lat=n_lat, cols=cols)

    w_pool_out = (P["pool_scale"][l][:, None] * P["pool_out"][l]).astype(BF16)
    merged = _merge([mixed, mla_o, win_o, diff_o],
                    [w_pool_out, P["mla_out"][l].astype(BF16), P["win_out"][l].astype(BF16),
                     P["diff_out"][l].astype(BF16)], gates, rows=rows)
    y1 = _mm(merged, P["w_out"][l].astype(BF16), rows=rows, name="out_proj")

    dense = l % 2 == 0
    h2, u2 = _resnorm(hres, P["norm2_g"][l], rows=rows, n_lat=n_lat, y=y1, gate=gt1, shift=sh2, scale=sc2,
                      emit_h=True, out_dtype=BF16 if dense else F32)
    j = l // 2
    if dense:
        y2 = _ffn(u2, P["ffn_w_gate"][j][None].astype(BF16), P["ffn_w_up"][j][None].astype(BF16),
                  P["ffn_w_down"][j][None].astype(BF16), rows=rows)
        return {"h": h2, "y": y2, "y_w": None, "gate": gt2}
    router = jnp.pad(P["moe_router"][j], ((0, 0), (0, LANES - N_EXPERTS)))
    logits = _mm(u2, router, rows=rows, out_dtype=F32, name="router")[:, :, :N_EXPERTS] + P["moe_router_b"][j]
    y2, top_w = _moe(u2.reshape(B * rows, D), logits.reshape(B * rows, N_EXPERTS), P["moe_w_gate"][j].astype(BF16),
                     P["moe_w_up"][j].astype(BF16), P["moe_w_down"][j].astype(BF16))
    return {"h": h2, "y": y2, "y_w": top_w.reshape(B, rows, TOP_K), "gate": gt2}


def kernel(x, c, ctx, c_ctx, w_mod, b_mod, norm1_g, norm2_g, w_in, pool_w, pool_scale, pool_out,
           mla_kv_norm_g, mla_w_kv_b, mla_out, win_sink, win_out, diff_lambda, diff_subln_g,
           diff_out, w_out, ffn_w_gate, ffn_w_up, ffn_w_down, moe_router, moe_router_b,
           moe_w_gate, moe_w_up, moe_w_down, final_norm_g):
    B, n_lat, D = x.shape
    n_ctx = ctx.shape[1]
    n_layers = w_in.shape[0]
    P = dict(norm1_g=norm1_g, norm2_g=norm2_g, w_in=w_in, pool_w=pool_w, pool_scale=pool_scale,
             pool_out=pool_out, mla_kv_norm_g=mla_kv_norm_g, mla_w_kv_b=mla_w_kv_b, mla_out=mla_out,
             win_sink=win_sink, win_out=win_out, diff_lambda=diff_lambda, diff_subln_g=diff_subln_g,
             diff_out=diff_out, w_out=w_out, ffn_w_gate=ffn_w_gate, ffn_w_up=ffn_w_up,
             ffn_w_down=ffn_w_down, moe_router=moe_router, moe_router_b=moe_router_b,
             moe_w_gate=moe_w_gate, moe_w_up=moe_w_up, moe_w_down=moe_w_down)
    tabs = _rope_tables(n_lat, n_ctx)

    cond = jnp.concatenate([c, c_ctx[None]], axis=0)
    cond = cond * (1.0 / (1.0 + jnp.exp(-cond)))
    n_pad = (-cond.shape[0]) % 8
    cond = jnp.pad(cond, ((0, n_pad), (0, 0)))[None]

    state = {"h": jnp.concatenate([x, ctx], axis=1), "y": None, "gate": None}
    for l in range(n_layers):
        mod = _mm(cond, w_mod[l], rows=cond.shape[1], out_dtype=F32, name="mod")[0, :B + 1] + b_mod[l]
        mods = [m.reshape(B + 1, 1, D) for m in jnp.split(mod, 6, axis=-1)]
        state = _layer(state, l, n_layers, P, mods, tabs, n_lat=n_lat)
    return _resnorm(state["h"], final_norm_g, rows=n_lat, n_lat=n_lat, y=state["y"], y_w=state["y_w"],
                    gate=state["gate"], out_dtype=F32)
```

```python
import functools
import math

import jax
import jax.numpy as jnp
from jax import lax
from jax.experimental import pallas as pl
from jax.experimental.pallas import tpu as pltpu

F32 = jnp.float32
BF16 = jnp.bfloat16

GRID_W = 64
ROPE_BASE = 10000.0
NORM_EPS = 1e-6
NEG_INF = -1e30
POOL_GROUPS = 4
POOL_WINDOWS = (2, 4, 8, 16)
MLA_HEADS = 8
MLA_NOPE = 128
MLA_ROPE = 64
MLA_V = 128
WIN_HEADS = 8
WIN_KV_HEADS = 2
WIN_HEAD_DIM = 128
WINDOW = 128
DIFF_HEADS = 4
DIFF_HEAD_DIM = 128
N_BRANCHES = 4
N_EXPERTS = 8
TOP_K = 2

LOG2E = math.log2(math.e)
LANES = 128
MXU_DIM = 256
VMEM_LIMIT = 56 * 1024 * 1024


def _tile(n, cap, mult):
    best = None
    for t in range(mult, min(n, cap) + 1, mult):
        if n % t == 0:
            best = t
    assert best is not None, (n, cap, mult)
    return best


def _params(sem):
    return pltpu.CompilerParams(dimension_semantics=sem, vmem_limit_bytes=VMEM_LIMIT)


def _resnorm_kernel(*refs, has_y, n_slots, has_mod, emit_h):
    it = iter(refs)
    h_ref = next(it)
    y_refs = [next(it) for _ in range(max(n_slots, 1))] if has_y else []
    yw_ref = next(it) if n_slots else None
    gate_ref = next(it) if has_y else None
    g_ref = next(it)
    sh_ref = next(it) if has_mod else None
    sc_ref = next(it) if has_mod else None
    hout_ref = next(it) if emit_h else None
    u_ref = next(it)
    h = h_ref[...]
    if has_y:
        if n_slots:
            y = sum(yw_ref[:, k:k + 1] * y_refs[k][...] for k in range(n_slots))
        else:
            y = y_refs[0][...].astype(F32)
        h = h + gate_ref[...] * y
    if emit_h:
        hout_ref[...] = h
    ms = jnp.mean(h * h, axis=-1, keepdims=True)
    v = h * lax.rsqrt(ms + NORM_EPS) * g_ref[...]
    if has_mod:
        v = v * (1.0 + sc_ref[...]) + sh_ref[...]
    u_ref[...] = v.astype(u_ref.dtype)


def _resnorm(h, norm_g, *, rows, n_lat, y=None, y_w=None, gate=None, shift=None, scale=None, emit_h=False,
             out_dtype=BF16):
    B, _, D = h.shape
    ctx = rows - n_lat
    tm = _tile(math.gcd(n_lat, ctx) if ctx else n_lat, 512, 16)
    n_lat_tiles = n_lat // tm
    has_y, has_mod = y is not None, shift is not None
    n_slots = 0 if y_w is None else y_w.shape[-1]
    row_spec = pl.BlockSpec((None, tm, D), lambda b, i: (b, i, 0))
    vec_spec = pl.BlockSpec((None, 1, D), lambda b, i: (jnp.where(i < n_lat_tiles, b, B), 0, 0))
    args, specs = [h], [row_spec]
    if n_slots:
        per_b = rows // tm
        args += [y] * n_slots + [y_w, gate]
        specs += [pl.BlockSpec((tm, D), lambda b, i, k=k: ((k * B + b) * per_b + i, 0)) for k in range(n_slots)]
        specs += [pl.BlockSpec((None, tm, n_slots), lambda b, i: (b, i, 0)), vec_spec]
    elif has_y:
        args += [y, gate]
        specs += [row_spec, vec_spec]
    args.append(norm_g.reshape(1, D))
    specs.append(pl.BlockSpec((1, D), lambda b, i: (0, 0)))
    if has_mod:
        args += [shift, scale]
        specs += [vec_spec, vec_spec]
    out_shape, out_specs = [], []
    if emit_h:
        out_shape.append(jax.ShapeDtypeStruct((B, rows, D), F32))
        out_specs.append(row_spec)
    out_shape.append(jax.ShapeDtypeStruct((B, rows, D), out_dtype))
    out_specs.append(row_spec)
    res = pl.pallas_call(
        functools.partial(_resnorm_kernel, has_y=has_y, n_slots=n_slots, has_mod=has_mod, emit_h=emit_h),
        grid=(B, rows // tm), in_specs=specs, out_specs=out_specs, out_shape=out_shape,
        compiler_params=_params(("parallel", "parallel")), name="resnorm")(*args)
    return res if emit_h else res[0]


def _mm_kernel(*refs, has_g, act):
    if has_g:
        a_ref, w_ref, g_ref, o_ref = refs
        af = a_ref[...].astype(F32)
        af = af * lax.rsqrt(jnp.mean(af * af, axis=-1, keepdims=True) + NORM_EPS) * g_ref[...]
        a = af.astype(BF16)
    else:
        a_ref, w_ref, o_ref = refs
        a = a_ref[...].astype(BF16)
    acc = jnp.dot(a, w_ref[...].astype(BF16), preferred_element_type=F32)
    if act == "sigmoid":
        acc = 1.0 / (1.0 + jnp.exp(-acc))
    o_ref[...] = acc.astype(o_ref.dtype)


def _mm(a, w, *, rows, k_block=0, k_width=None, rms_g=None, act=None, out_dtype=BF16,
        tm_cap=1100, tn_cap=1024, name="mm"):
    B = a.shape[0]
    K, N = w.shape
    if k_width is None:
        assert a.shape[2] == K
    tm = _tile(rows, tm_cap, 8 if rows < 16 else 16)
    tn = _tile(N, tn_cap, MXU_DIM if N % MXU_DIM == 0 else LANES)
    args = [a, w]
    specs = [pl.BlockSpec((None, tm, K), lambda b, i, j: (b, i, k_block)),
             pl.BlockSpec((K, tn), lambda b, i, j: (0, j))]
    if rms_g is not None:
        args.append(rms_g.reshape(1, K).astype(F32))
        specs.append(pl.BlockSpec((1, K), lambda b, i, j: (0, 0)))
    return pl.pallas_call(
        functools.partial(_mm_kernel, has_g=rms_g is not None, act=act),
        grid=(B, rows // tm, N // tn), in_specs=specs,
        out_specs=pl.BlockSpec((None, tm, tn), lambda b, i, j: (b, i, j)),
        out_shape=jax.ShapeDtypeStruct((B, rows, N), out_dtype),
        compiler_params=_params(("parallel", "parallel", "arbitrary")), name=name)(*args)


def _mm_rope_kernel(a_ref, w_ref, cos_ref, sin_hi_ref, sin_lo_ref, o_ref, *, half):
    acc = jnp.dot(a_ref[...], w_ref[...], preferred_element_type=F32)
    cos, sin_hi, sin_lo = cos_ref[...], sin_hi_ref[...], sin_lo_ref[...]
    for hd in range(acc.shape[1] // LANES):
        x = acc[:, hd * LANES:(hd + 1) * LANES]
        y = x * cos + pltpu.roll(x, half, 1) * sin_hi + pltpu.roll(x, LANES - half, 1) * sin_lo
        o_ref[:, hd * LANES:(hd + 1) * LANES] = y.astype(o_ref.dtype)


def _mm_rope(a, w, tables, *, half, rows, tn_cap, name):
    B, _, K = a.shape
    N = w.shape[1]
    tm = _tile(rows, 1100, 16)
    tn = _tile(N, tn_cap, LANES)
    tab_spec = pl.BlockSpec((tm, LANES), lambda b, i, j: (i, 0))
    return pl.pallas_call(
        functools.partial(_mm_rope_kernel, half=half), grid=(B, rows // tm, N // tn),
        in_specs=[pl.BlockSpec((None, tm, K), lambda b, i, j: (b, i, 0)),
                  pl.BlockSpec((K, tn), lambda b, i, j: (0, j)), tab_spec, tab_spec, tab_spec],
        out_specs=pl.BlockSpec((None, tm, tn), lambda b, i, j: (b, i, j)),
        out_shape=jax.ShapeDtypeStruct((B, rows, N), BF16),
        compiler_params=_params(("parallel", "parallel", "arbitrary")), name=name)(a, w, *tables)


POOL_HALO = 16


def _pool_kernel(prev_ref, x_ref, next_ref, w_ref, o_ref, *, n_lat, n_rows):
    tm = x_ref.shape[0]
    i = pl.program_id(1)
    half_w = jnp.left_shift(1, pl.program_id(2))
    xh = jnp.concatenate([prev_ref[...], x_ref[...], next_ref[...]], axis=0)
    shape = (tm, tm + 2 * POOL_HALO)
    row0 = i * tm
    r = lax.broadcasted_iota(jnp.int32, shape, 0) + row0
    src = lax.broadcasted_iota(jnp.int32, shape, 1) + (row0 - POOL_HALO)
    latent = row0 < n_lat
    lo = jnp.maximum(r - half_w, jnp.where(latent, 0, n_lat))
    hi = jnp.minimum(r + half_w, jnp.where(latent, n_lat, n_rows))
    band = jnp.where((src >= lo) & (src < hi), 1.0 / (hi - lo).astype(F32), 0.0) - jnp.where(src == r, 1.0, 0.0)
    pooled = jnp.dot(band.astype(BF16), xh, preferred_element_type=F32)
    o_ref[...] = jnp.dot(pooled.astype(BF16), w_ref[...], preferred_element_type=F32).astype(o_ref.dtype)


def _pool_mix(z, w, *, rows, n_lat):
    B, T, _ = z.shape
    G, gd, _ = w.shape
    assert tuple(2 << g for g in range(G)) == POOL_WINDOWS and max(POOL_WINDOWS) // 2 <= POOL_HALO
    ctx = rows - n_lat
    tm = _tile(math.gcd(n_lat, ctx) if ctx else n_lat, 512, POOL_HALO)
    per = tm // POOL_HALO
    last = T // POOL_HALO - 1
    return pl.pallas_call(
        functools.partial(_pool_kernel, n_lat=n_lat, n_rows=rows), grid=(B, rows // tm, G),
        in_specs=[pl.BlockSpec((None, POOL_HALO, gd), lambda b, i, g: (b, jnp.maximum(i * per - 1, 0), g)),
                  pl.BlockSpec((None, tm, gd), lambda b, i, g: (b, i, g)),
                  pl.BlockSpec((None, POOL_HALO, gd), lambda b, i, g: (b, jnp.minimum((i + 1) * per, last), g)),
                  pl.BlockSpec((None, gd, gd), lambda b, i, g: (g, 0, 0))],
        out_specs=pl.BlockSpec((None, tm, gd), lambda b, i, g: (b, i, g)),
        out_shape=jax.ShapeDtypeStruct((B, rows, G * gd), BF16),
        compiler_params=_params(("parallel", "parallel", "arbitrary")), name="pool_mix")(z, z, z, w)


def _merge_kernel(a0, a1, a2, a3, w0, w1, w2, w3, g0, g1, g2, g3, o_ref):
    acc = None
    for a_ref, w_ref, g_ref in ((a0, w0, g0), (a1, w1, g1), (a2, w2, g2), (a3, w3, g3)):
        y = jnp.dot(a_ref[...], w_ref[...], preferred_element_type=F32) * g_ref[...].astype(F32)
        acc = y if acc is None else acc + y
    o_ref[...] = acc.astype(o_ref.dtype)


def _merge(branches, weights, gates, *, rows):
    B = branches[0].shape[0]
    N = weights[0].shape[1]
    tm = _tile(rows, 1100, 16)
    tn = _tile(N, 512, LANES)
    nj = N // tn
    a_specs = [pl.BlockSpec((None, tm, w.shape[0]), lambda b, i, j: (b, i, 0)) for w in weights]
    w_specs = [pl.BlockSpec((w.shape[0], tn), lambda b, i, j: (0, j)) for w in weights]
    g_specs = [pl.BlockSpec((None, tm, tn), lambda b, i, j, br=br: (b, i, br * nj + j))
               for br in range(N_BRANCHES)]
    return pl.pallas_call(
        _merge_kernel, grid=(B, rows // tm, nj),
        in_specs=a_specs + w_specs + g_specs,
        out_specs=pl.BlockSpec((None, tm, tn), lambda b, i, j: (b, i, j)),
        out_shape=jax.ShapeDtypeStruct((B, rows, N), BF16),
        compiler_params=_params(("parallel", "parallel", "arbitrary")),
        name="merge")(*branches, *weights, gates, gates, gates, gates)


def _ffn_kernel(*refs, has_rs, n_steps):
    if has_rs:
        a_ref, wg_ref, wu_ref, wd_ref, rs_ref, o_ref, acc_ref = refs
    else:
        a_ref, wg_ref, wu_ref, wd_ref, o_ref, acc_ref = refs
    k = pl.program_id(2)

    @pl.when(k == 0)
    def _():
        acc_ref[...] = jnp.zeros_like(acc_ref)

    a = a_ref[...]
    g = jnp.dot(a, wg_ref[...], preferred_element_type=F32)
    u = jnp.dot(a, wu_ref[...], preferred_element_type=F32)
    hid = g * (1.0 / (1.0 + jnp.exp(-g))) * u
    if has_rs:
        hid = hid * rs_ref[...]
    acc_ref[...] += jnp.dot(hid.astype(BF16), wd_ref[...], preferred_element_type=F32)

    @pl.when(k == n_steps - 1)
    def _():
        o_ref[...] = acc_ref[...].astype(o_ref.dtype)


def _ffn(a, wg, wu, wd, *, rows, row_scale=None):
    B, _, D = a.shape
    E, _, F = wg.shape
    tm = _tile(rows, 1100, 16)
    tf = _tile(F, 512, LANES)
    nf = F // tf
    n_steps = E * nf
    args = [a, wg, wu, wd]
    specs = [pl.BlockSpec((None, tm, D), lambda b, i, k: (b, i, 0)),
             pl.BlockSpec((None, D, tf), lambda b, i, k: (k // nf, 0, k % nf)),
             pl.BlockSpec((None, D, tf), lambda b, i, k: (k // nf, 0, k % nf)),
             pl.BlockSpec((None, tf, D), lambda b, i, k: (k // nf, k % nf, 0))]
    if row_scale is not None:
        args.append(row_scale)
        specs.append(pl.BlockSpec((None, None, tm, 1), lambda b, i, k: (b, k // nf, i, 0)))
    return pl.pallas_call(
        functools.partial(_ffn_kernel, has_rs=row_scale is not None, n_steps=n_steps),
        grid=(B, rows // tm, n_steps), in_specs=specs,
        out_specs=pl.BlockSpec((None, tm, D), lambda b, i, k: (b, i, 0)),
        out_shape=jax.ShapeDtypeStruct((B, rows, D), BF16),
        scratch_shapes=[pltpu.VMEM((tm, D), F32)],
        compiler_params=_params(("parallel", "parallel", "arbitrary")), name="ffn")(*args)


def _moe_kernel(te_ref, nu_ref, x_hbm, ord_hbm, wg_ref, wu_ref, wd_ref, y_hbm,
                gbuf, xb, acc, idx, sem_g, sem_s, sem_i, *, tm, nf, n_tok):
    del te_ref
    t = pl.program_id(0)
    f = pl.program_id(1)
    n_used = nu_ref[0]
    slot = t % 2

    def idx_copy(tile, s):
        return pltpu.make_async_copy(ord_hbm.at[pl.ds(tile * (2 * tm), 2 * tm)], idx.at[s], sem_i)

    def gather_copy(tok, r):
        return pltpu.make_async_copy(x_hbm.at[pl.ds(tok, 1)], gbuf.at[pl.ds(r, 1)], sem_g)

    def scatter_copy(s, r, dst):
        return pltpu.make_async_copy(acc.at[s, pl.ds(r, 1)], y_hbm.at[pl.ds(dst, 1)], sem_s.at[s])

    def for_rows(fn):
        def body(i, c):
            base = pl.multiple_of(i * SUBLANES, SUBLANES)
            for j in range(SUBLANES):
                fn(base + j)
            return c
        lax.fori_loop(0, tm // SUBLANES, body, 0)

    def issue_gather(s):
        for_rows(lambda r: gather_copy(idx[s, r], r).start())

    def wait_gather():
        pltpu.make_async_copy(x_hbm.at[pl.ds(0, tm)], gbuf, sem_g).wait()

    def issue_scatter(s):
        for_rows(lambda r: scatter_copy(s, r, idx[s, tm + r]).start())

    def wait_scatter(s):
        pltpu.make_async_copy(acc.at[s], y_hbm.at[pl.ds(0, tm)], sem_s.at[s]).wait()

    @pl.when(t < n_used)
    def _():
        @pl.when(f == 0)
        def _():
            @pl.when(t == 0)
            def _():
                first = idx_copy(0, 0)
                first.start()
                first.wait()
                issue_gather(0)

            wait_gather()
            xb[...] = gbuf[...].astype(BF16)

            @pl.when(t + 1 < n_used)
            def _():
                idx_copy(t + 1, 1 - slot).start()

        @pl.when((f == 1) & (t + 1 < n_used))
        def _():
            idx_copy(t + 1, 1 - slot).wait()
            issue_gather(1 - slot)

        a = xb[...]
        g = jnp.dot(a, wg_ref[...], preferred_element_type=F32)
        u = jnp.dot(a, wu_ref[...], preferred_element_type=F32)
        hid = (g * (1.0 / (1.0 + jnp.exp(-g))) * u).astype(BF16)
        part = jnp.dot(hid, wd_ref[...], preferred_element_type=F32)

        @pl.when(f == 0)
        def _():
            acc[slot] = part

        @pl.when(f > 0)
        def _():
            acc[slot] += part

        @pl.when(f == nf - 1)
        def _():
            issue_scatter(slot)

            @pl.when(t > 0)
            def _():
                wait_scatter(1 - slot)

            @pl.when(t == n_used - 1)
            def _():
                wait_scatter(slot)


MOE_TILE_ROWS = 1024
SUBLANES = 8


def _moe_route(logits, tm):
    n_tok = logits.shape[0]
    top_v, top_i = lax.top_k(logits, TOP_K)
    top_w = jax.nn.softmax(top_v, axis=-1)
    e_flat = top_i.reshape(-1).astype(jnp.int32)
    n_asg = e_flat.shape[0]
    cnt = jnp.sum(jax.nn.one_hot(e_flat, N_EXPERTS, dtype=jnp.int32), axis=0)
    pad = (-cnt) % tm
    big = 2 * N_EXPERTS
    j = jnp.arange(tm, dtype=jnp.int32)[None, :]
    e = jnp.arange(N_EXPERTS, dtype=jnp.int32)[:, None]
    pad_keys = jnp.where(j < pad[:, None], 2 * e + 1, big).reshape(-1)
    keys = jnp.concatenate([2 * e_flat, pad_keys])
    vals = jnp.concatenate([jnp.arange(n_asg, dtype=jnp.int32), jnp.full((N_EXPERTS * tm,), -1, jnp.int32)])
    keys, order = lax.sort_key_val(keys, vals)
    n_tiles = keys.shape[0] // tm
    tile_key = keys.reshape(n_tiles, tm)[:, 0]
    used = tile_key < big
    n_used = jnp.sum(used).astype(jnp.int32)
    tile_e = jnp.where(used, tile_key // 2, 0)
    tile_e = jnp.where(used, tile_e, tile_e[jnp.maximum(n_used - 1, 0)])
    src = jnp.maximum(order, 0) // TOP_K
    spare = TOP_K * n_tok + jnp.arange(order.shape[0], dtype=jnp.int32) % tm
    dst = jnp.where(order >= 0, (order % TOP_K) * n_tok + order // TOP_K, spare)
    rows = jnp.concatenate([src.reshape(n_tiles, tm), dst.reshape(n_tiles, tm)], axis=1).reshape(-1)
    return top_w, rows, tile_e.astype(jnp.int32), n_used.reshape(1)


def _moe(x, logits, wg, wu, wd):
    n_tok, D = x.shape
    E, _, F = wg.shape
    tm = _tile(n_tok, MOE_TILE_ROWS, 16)
    tf = _tile(F, 512, LANES)
    nf = F // tf
    assert nf >= 2
    top_w, order, tile_e, n_used = _moe_route(logits, tm)
    n_tiles = order.shape[0] // (2 * tm)

    def w_in_map(t, f, te, nu):
        return (te[t], 0, jnp.where(t < nu[0], f, nf - 1))

    def w_out_map(t, f, te, nu):
        return (te[t], jnp.where(t < nu[0], f, nf - 1), 0)

    grid_spec = pltpu.PrefetchScalarGridSpec(
        num_scalar_prefetch=2, grid=(n_tiles, nf),
        in_specs=[pl.BlockSpec(memory_space=pl.ANY), pl.BlockSpec(memory_space=pl.ANY),
                  pl.BlockSpec((None, D, tf), w_in_map), pl.BlockSpec((None, D, tf), w_in_map),
                  pl.BlockSpec((None, tf, D), w_out_map)],
        out_specs=pl.BlockSpec(memory_space=pl.ANY),
        scratch_shapes=[pltpu.VMEM((tm, D), F32), pltpu.VMEM((tm, D), BF16), pltpu.VMEM((2, tm, D), F32),
                        pltpu.SMEM((2, 2 * tm), jnp.int32), pltpu.SemaphoreType.DMA(()),
                        pltpu.SemaphoreType.DMA((2,)), pltpu.SemaphoreType.DMA(())])
    y = pl.pallas_call(
        functools.partial(_moe_kernel, tm=tm, nf=nf, n_tok=n_tok), grid_spec=grid_spec,
        out_shape=jax.ShapeDtypeStruct((TOP_K * n_tok + tm, D), F32),
        compiler_params=_params(("arbitrary", "arbitrary")), name="moe")(
            tile_e, n_used, x, order, wg, wu, wd)
    return y, top_w


ATTN_CHAIN_ROWS = 256


def _softmax_rows(s):
    p = jnp.exp2(s - jnp.max(s, axis=-1, keepdims=True))
    return p.astype(BF16), jnp.sum(p, axis=-1, keepdims=True)


def _chains(tq):
    r = min(tq, ATTN_CHAIN_ROWS)
    return [slice(a * r, (a + 1) * r) for a in range(tq // r)]


def _qk(q, k):
    return lax.dot_general(q, k, (((1,), (1,)), ((), ())), preferred_element_type=F32)


def _mla_kernel(qn_ref, qr_ref, kn_ref, kr_ref, v_ref, *rest):
    o_ref = rest[-1]
    k = jnp.concatenate([kn_ref[...], kr_ref[...]], axis=-1)
    v = v_ref[...]
    for rows in _chains(qn_ref.shape[0]):
        q = jnp.concatenate([qn_ref[rows, :], qr_ref[rows, :]], axis=-1)
        p, l = _softmax_rows(_qk(q, k))
        o_ref[rows, :] = (jnp.dot(p, v, preferred_element_type=F32) / l).astype(o_ref.dtype)


def _attn_calls(kernel, make_specs, args, out_width, vec_args, *, B, H, T, n_lat, need_ctx, tq_cap, name):
    ctx = T - n_lat
    out_rows = T if need_ctx else n_lat

    def call(tq, q_blk0, nq, k_rows, k_blk, prev):
        specs = make_specs(tq, q_blk0, k_rows, k_blk)
        specs += [pl.BlockSpec(v.shape, lambda b, h, i: (0, 0)) for v in vec_args]
        ins = list(args) + list(vec_args)
        aliases = {}
        if prev is not None:
            specs.append(pl.BlockSpec(memory_space=pl.ANY))
            aliases = {len(ins): 0}
            ins.append(prev)
        return pl.pallas_call(
            kernel, grid=(B, H, nq), in_specs=specs,
            out_specs=pl.BlockSpec((None, tq, out_width), lambda b, h, i: (b, q_blk0 + i, h)),
            out_shape=jax.ShapeDtypeStruct((B, out_rows, H * out_width), BF16),
            input_output_aliases=aliases,
            compiler_params=_params(("parallel", "parallel", "arbitrary")), name=name)(*ins)

    tq = _tile(n_lat, tq_cap, ATTN_CHAIN_ROWS)
    out = call(tq, 0, n_lat // tq, T, 0, None)
    if need_ctx:
        out = call(ctx, n_lat // ctx, 1, ctx, n_lat // ctx, out)
    return out


def _mla_attn(z, zr, kv, *, n_lat, need_ctx, cols):
    B, T, _ = z.shape
    H = MLA_HEADS
    qn0, qr0, kr0 = cols["q_nope"], cols["q_rope"], cols["k_rope"]

    def specs(tq, q_blk0, k_rows, k_blk):
        return [pl.BlockSpec((None, tq, LANES), lambda b, h, i: (b, q_blk0 + i, qn0 + h)),
                pl.BlockSpec((None, tq, LANES), lambda b, h, i: (b, q_blk0 + i, qr0 + h)),
                pl.BlockSpec((None, k_rows, LANES), lambda b, h, i: (b, k_blk, h)),
                pl.BlockSpec((None, k_rows, LANES), lambda b, h, i: (b, k_blk, kr0)),
                pl.BlockSpec((None, k_rows, LANES), lambda b, h, i: (b, k_blk, H + h))]

    return _attn_calls(_mla_kernel, specs, (z, zr, kv, zr, kv), MLA_V, (), B=B, H=H, T=T, n_lat=n_lat,
                       need_ctx=need_ctx, tq_cap=2048, name="mla_attn")


def _diff_kernel(q_ref, k_ref, v_ref, lam_ref, g_ref, *rest):
    o_ref = rest[-1]
    d = DIFF_HEAD_DIM
    k = k_ref[...]
    k1, k2 = k[:, :d], k[:, d:]
    v = v_ref[...]
    for rows in _chains(q_ref.shape[0]):
        q = q_ref[rows, :]
        p1, l1 = _softmax_rows(_qk(q[:, :d], k1))
        p2, l2 = _softmax_rows(_qk(q[:, d:], k2))
        o = (jnp.dot(p1, v, preferred_element_type=F32) / l1
             - lam_ref[...] * (jnp.dot(p2, v, preferred_element_type=F32) / l2))
        o = o * lax.rsqrt(jnp.mean(o * o, axis=-1, keepdims=True) + NORM_EPS) * g_ref[...]
        o_ref[rows, :] = o.astype(o_ref.dtype)


def _diff_attn(z, zr, lam_vec, g_vec, *, n_lat, need_ctx, cols):
    B, T, _ = z.shape
    dd = 2 * DIFF_HEAD_DIM
    q0, k0, v0 = cols["q_diff"], cols["k_diff"], cols["v_diff"]

    def specs(tq, q_blk0, k_rows, k_blk):
        return [pl.BlockSpec((None, tq, dd), lambda b, h, i: (b, q_blk0 + i, q0 + h)),
                pl.BlockSpec((None, k_rows, dd), lambda b, h, i: (b, k_blk, k0 + h)),
                pl.BlockSpec((None, k_rows, dd), lambda b, h, i: (b, k_blk, v0 + h))]

    return _attn_calls(_diff_kernel, specs, (zr, zr, z), dd, (lam_vec, g_vec), B=B, H=DIFF_HEADS, T=T,
                       n_lat=n_lat, need_ctx=need_ctx, tq_cap=1024, name="diff_attn")


def _win_kernel(q_ref, kp_ref, kc_ref, kn_ref, kx_ref, vp_ref, vc_ref, vn_ref, vx_ref, sink_ref,
                o_ref, *, n_lat):
    d = WIN_HEAD_DIM
    tq = q_ref.shape[0]
    i = pl.program_id(2)
    k = jnp.concatenate([kp_ref[...], kc_ref[...], kn_ref[...], kx_ref[...]], axis=0)
    v = jnp.concatenate([vp_ref[...], vc_ref[...], vn_ref[...], vx_ref[...]], axis=0)
    n_loc = tq + 2 * WINDOW
    nk = k.shape[0]
    r = lax.broadcasted_iota(jnp.int32, (tq, nk), 0)
    c = lax.broadcasted_iota(jnp.int32, (tq, nk), 1)
    kpos = i * tq - WINDOW + c
    rel = c - WINDOW - r
    valid = (c >= n_loc) | ((jnp.abs(rel) <= WINDOW) & (kpos >= 0) & (kpos < n_lat))
    outs = []
    for g in range(WIN_HEADS // WIN_KV_HEADS):
        q = q_ref[:, g * d:(g + 1) * d]
        s = lax.dot_general(q, k, (((1,), (1,)), ((), ())), preferred_element_type=F32)
        s = jnp.where(valid, s, NEG_INF)
        sink = sink_ref[:, g * d:g * d + 1]
        m = jnp.maximum(jnp.max(s, axis=-1, keepdims=True), sink)
        p = jnp.exp2(s - m)
        l = jnp.sum(p, axis=-1, keepdims=True) + jnp.exp2(sink - m)
        o = jnp.dot(p.astype(BF16), v, preferred_element_type=F32)
        outs.append(o / l)
    o_ref[...] = jnp.concatenate(outs, axis=-1).astype(o_ref.dtype)


def _win_attn(z, zr, sink_vec, *, n_lat, out_rows, cols):
    B, T, _ = z.shape
    d = WIN_HEAD_DIM
    grp = WIN_HEADS // WIN_KV_HEADS
    ctx = T - n_lat
    tq = _tile(n_lat, 512, WINDOW)
    per = tq // WINDOW
    q0, k0, v0 = cols["q_win"], cols["k_win"], cols["v_win"]

    def halo(col0):
        return [pl.BlockSpec((None, WINDOW, d), lambda b, h, i: (b, jnp.maximum(i * per - 1, 0), col0 + h)),
                pl.BlockSpec((None, tq, d), lambda b, h, i: (b, i, col0 + h)),
                pl.BlockSpec((None, WINDOW, d), lambda b, h, i: (b, (i + 1) * per, col0 + h)),
                pl.BlockSpec((None, ctx, d), lambda b, h, i: (b, n_lat // ctx, col0 + h))]

    return pl.pallas_call(
        functools.partial(_win_kernel, n_lat=n_lat), grid=(B, WIN_KV_HEADS, n_lat // tq),
        in_specs=[pl.BlockSpec((None, tq, grp * d), lambda b, h, i: (b, i, q0 + h))]
        + halo(k0) + halo(v0) + [pl.BlockSpec((1, grp * d), lambda b, h, i: (0, h))],
        out_specs=pl.BlockSpec((None, tq, grp * d), lambda b, h, i: (b, i, h)),
        out_shape=jax.ShapeDtypeStruct((B, out_rows, WIN_HEADS * d), BF16),
        compiler_params=_params(("parallel", "parallel", "arbitrary")),
        name="win_attn")(zr, zr, zr, zr, zr, z, z, z, z, sink_vec)


def _sink_kernel(q_ref, k_ref, v_ref, sink_ref, prev_ref, o_ref):
    del prev_ref
    d = WIN_HEAD_DIM
    k = k_ref[...]
    v = v_ref[...]
    outs = []
    for g in range(WIN_HEADS // WIN_KV_HEADS):
        q = q_ref[:, g * d:(g + 1) * d]
        s = lax.dot_general(q, k, (((1,), (1,)), ((), ())), preferred_element_type=F32)
        sink = sink_ref[:, g * d:g * d + 1]
        m = jnp.maximum(jnp.max(s, axis=-1, keepdims=True), sink)
        p = jnp.exp2(s - m)
        l = jnp.sum(p, axis=-1, keepdims=True) + jnp.exp2(sink - m)
        outs.append(jnp.dot(p.astype(BF16), v, preferred_element_type=F32) / l)
    o_ref[...] = jnp.concatenate(outs, axis=-1).astype(o_ref.dtype)


def _sink_attn(z, zr, sink_vec, prev, *, n_lat, cols):
    B, T, _ = z.shape
    d = WIN_HEAD_DIM
    grp = WIN_HEADS // WIN_KV_HEADS
    ctx = T - n_lat
    blk = n_lat // ctx
    q0, k0, v0 = cols["q_win"], cols["k_win"], cols["v_win"]
    return pl.pallas_call(
        _sink_kernel, grid=(B, WIN_KV_HEADS),
        in_specs=[pl.BlockSpec((None, ctx, grp * d), lambda b, h: (b, blk, q0 + h)),
                  pl.BlockSpec((None, ctx, d), lambda b, h: (b, blk, k0 + h)),
                  pl.BlockSpec((None, ctx, d), lambda b, h: (b, blk, v0 + h)),
                  pl.BlockSpec((1, grp * d), lambda b, h: (0, h)),
                  pl.BlockSpec(memory_space=pl.ANY)],
        out_specs=pl.BlockSpec((None, ctx, grp * d), lambda b, h: (b, blk, h)),
        out_shape=jax.ShapeDtypeStruct(prev.shape, prev.dtype),
        input_output_aliases={4: 0},
        compiler_params=_params(("parallel", "parallel")), name="sink_attn")(zr, zr, z, sink_vec, prev)


def _in_proj_layout(D):
    pool_w = POOL_GROUPS * (D // 8)
    widths = [("pool", pool_w), ("q_mla", MLA_HEADS * (MLA_NOPE + MLA_ROPE)), ("ckv", D // 4),
              ("k_rope", MLA_ROPE), ("q_win", WIN_HEADS * WIN_HEAD_DIM),
              ("k_win", WIN_KV_HEADS * WIN_HEAD_DIM), ("v_win", WIN_KV_HEADS * WIN_HEAD_DIM),
              ("q_diff", DIFF_HEADS * 2 * DIFF_HEAD_DIM), ("k_diff", DIFF_HEADS * 2 * DIFF_HEAD_DIM),
              ("v_diff", DIFF_HEADS * 2 * DIFF_HEAD_DIM), ("gates", N_BRANCHES * D)]
    out, off = {}, 0
    for name, w in widths:
        out[name] = (off, w)
        off += w
    return out


def _prep_in_proj(w_in, D):
    lay = _in_proj_layout(D)

    def cols(name):
        o, w = lay[name]
        return w_in[:, o:o + w]

    q_mla = cols("q_mla").reshape(D, MLA_HEADS, MLA_NOPE + MLA_ROPE) * (LOG2E * (MLA_NOPE + MLA_ROPE) ** -0.5)
    q_nope = q_mla[:, :, :MLA_NOPE].reshape(D, MLA_HEADS * MLA_NOPE)
    q_rope = jnp.pad(q_mla[:, :, MLA_NOPE:], ((0, 0), (0, 0), (0, LANES - MLA_ROPE))).reshape(D, MLA_HEADS * LANES)
    k_rope = jnp.pad(cols("k_rope"), ((0, 0), (0, LANES - MLA_ROPE)))
    plain = [cols("pool"), q_nope, cols("ckv"), cols("v_win"), cols("v_diff")]
    rot = [cols("q_win") * (LOG2E * WIN_HEAD_DIM ** -0.5), cols("q_diff") * (LOG2E * DIFF_HEAD_DIM ** -0.5),
           cols("k_win"), cols("k_diff")]
    rot_mla = [q_rope, k_rope]
    n_mla = sum(x.shape[1] for x in rot_mla)
    rot_mla.append(jnp.zeros((D, (-n_mla) % MXU_DIM), w_in.dtype))
    cat = lambda parts: jnp.concatenate(parts, axis=1).astype(BF16)
    return cat(plain), cat(rot), cat(rot_mla), cols("gates").astype(BF16)


def _rope_tables(n_lat, ctx):
    t = jnp.arange(n_lat)
    rows, cols_ = (t // GRID_W).astype(F32), (t % GRID_W).astype(F32)

    def table(half, pad):
        inv = ROPE_BASE ** (-jnp.arange(half, dtype=F32) / half)
        ar, ac = rows[:, None] * inv[None, :], cols_[:, None] * inv[None, :]
        zero = jnp.zeros_like(ar)
        cos = jnp.concatenate([jnp.cos(ar)] * 2 + [jnp.cos(ac)] * 2, axis=1)
        sin_hi = jnp.concatenate([zero, jnp.sin(ar), zero, jnp.sin(ac)], axis=1)
        sin_lo = jnp.concatenate([-jnp.sin(ar), zero, -jnp.sin(ac), zero], axis=1)
        cos = jnp.pad(cos, ((0, ctx), (0, pad)), constant_values=1.0)
        return cos, jnp.pad(sin_hi, ((0, ctx), (0, pad))), jnp.pad(sin_lo, ((0, ctx), (0, pad)))

    return table(WIN_HEAD_DIM // 4, 0), table(MLA_ROPE // 4, LANES - MLA_ROPE)


def _layer(h, l, n_layers, P, mods, tabs, *, n_lat):
    B, T, D = h["h"].shape
    need_ctx = l < n_layers - 1
    rows = T if need_ctx else n_lat
    lam_init = 0.8 - 0.6 * math.exp(-0.3 * l)
    sh1, sc1, gt1, sh2, sc2, gt2 = mods

    if h["y"] is None:
        u = _resnorm(h["h"], P["norm1_g"][l], rows=T, n_lat=n_lat, shift=sh1, scale=sc1)
        hres = h["h"]
    else:
        hres, u = _resnorm(h["h"], P["norm1_g"][l], rows=T, n_lat=n_lat, y=h["y"], y_w=h["y_w"], gate=h["gate"],
                           shift=sh1, scale=sc1, emit_h=True)

    w_plain, w_rot, w_rot_mla, w_gate = _prep_in_proj(P["w_in"][l], D)
    z = _mm(u, w_plain, rows=T, tn_cap=1536, name="in_proj")
    zr = _mm_rope(u, w_rot, tabs[0], half=WIN_HEAD_DIM // 4, rows=T, tn_cap=1792, name="in_proj_rot")
    zm = _mm_rope(u, w_rot_mla, tabs[1], half=MLA_ROPE // 4, rows=T, tn_cap=1792, name="in_proj_rot_mla")
    gates = _mm(u, w_gate, rows=rows, act="sigmoid", name="gate_proj")

    gd = D // 8
    pool_w = POOL_GROUPS * gd
    kvr = D // 4
    o_qn, o_ckv = pool_w, pool_w + MLA_HEADS * MLA_NOPE
    o_vw = o_ckv + kvr
    o_vd = o_vw + WIN_KV_HEADS * WIN_HEAD_DIM
    r_qd = WIN_HEADS * WIN_HEAD_DIM
    r_kw = r_qd + DIFF_HEADS * 2 * DIFF_HEAD_DIM
    r_kd = r_kw + WIN_KV_HEADS * WIN_HEAD_DIM
    dd = 2 * DIFF_HEAD_DIM
    for off, blk in ((o_qn, LANES), (o_ckv, kvr), (o_vw, LANES), (o_vd, dd), (r_qd, dd), (r_kw, LANES), (r_kd, dd)):
        assert off % blk == 0, (off, blk)
    cols = {"q_nope": o_qn // LANES, "q_rope": 0, "k_rope": MLA_HEADS,
            "q_diff": r_qd // dd, "k_diff": r_kd // dd, "v_diff": o_vd // dd,
            "q_win": 0, "k_win": r_kw // LANES, "v_win": o_vw // LANES}

    wkv = P["mla_w_kv_b"][l].reshape(kvr, MLA_HEADS, MLA_NOPE + MLA_V)
    wkv = jnp.concatenate([wkv[:, :, :MLA_NOPE].reshape(kvr, -1), wkv[:, :, MLA_NOPE:].reshape(kvr, -1)],
                          axis=1).astype(BF16)
    kv = _mm(z, wkv, rows=T, k_block=o_ckv // kvr, k_width=kvr, rms_g=P["mla_kv_norm_g"][l], name="kv_proj")

    mixed = _pool_mix(z, P["pool_w"][l].astype(BF16), rows=rows, n_lat=n_lat)

    mla_o = _mla_attn(z, zm, kv, n_lat=n_lat, need_ctx=need_ctx, cols=cols)
    lp = P["diff_lambda"][l].astype(F32)
    lam = jnp.exp(jnp.sum(lp[0] * lp[1])) - jnp.exp(jnp.sum(lp[2] * lp[3])) + lam_init
    lam_vec = jnp.full((1, dd), lam, F32)
    g_vec = (P["diff_subln_g"][l].astype(F32) * (1.0 - lam_init)).reshape(1, dd)
    diff_o = _diff_attn(z, zr, lam_vec, g_vec, n_lat=n_lat, need_ctx=need_ctx, cols=cols)
    sink_vec = jnp.repeat(P["win_sink"][l].astype(F32) * LOG2E, WIN_HEAD_DIM).reshape(1, WIN_HEADS * WIN_HEAD_DIM)
    win_o = _win_attn(z, zr, sink_vec, n_lat=n_lat, out_rows=rows, cols=cols)
    if need_ctx:
        win_o = _sink_attn(z, zr, sink_vec, win_o, n_lat=n_lat, cols=cols)

    w_pool_out = (P["pool_scale"][l][:, None] * P["pool_out"][l]).astype(BF16)
    merged = _merge([mixed, mla_o, win_o, diff_o],
                    [w_pool_out, P["mla_out"][l].astype(BF16), P["win_out"][l].astype(BF16),
                     P["diff_out"][l].astype(BF16)], gates, rows=rows)
    y1 = _mm(merged, P["w_out"][l].astype(BF16), rows=rows, name="out_proj")

    dense = l % 2 == 0
    h2, u2 = _resnorm(hres, P["norm2_g"][l], rows=rows, n_lat=n_lat, y=y1, gate=gt1, shift=sh2, scale=sc2,
                      emit_h=True, out_dtype=BF16 if dense else F32)
    j = l // 2
    if dense:
        y2 = _ffn(u2, P["ffn_w_gate"][j][None].astype(BF16), P["ffn_w_up"][j][None].astype(BF16),
                  P["ffn_w_down"][j][None].astype(BF16), rows=rows)
        return {"h": h2, "y": y2, "y_w": None, "gate": gt2}
    router = jnp.pad(P["moe_router"][j], ((0, 0), (0, LANES - N_EXPERTS)))
    logits = _mm(u2, router, rows=rows, out_dtype=F32, name="router")[:, :, :N_EXPERTS] + P["moe_router_b"][j]
    y2, top_w = _moe(u2.reshape(B * rows, D), logits.reshape(B * rows, N_EXPERTS), P["moe_w_gate"][j].astype(BF16),
                     P["moe_w_up"][j].astype(BF16), P["moe_w_down"][j].astype(BF16))
    return {"h": h2, "y": y2, "y_w": top_w.reshape(B, rows, TOP_K), "gate": gt2}


def kernel(x, c, ctx, c_ctx, w_mod, b_mod, norm1_g, norm2_g, w_in, pool_w, pool_scale, pool_out,
           mla_kv_norm_g, mla_w_kv_b, mla_out, win_sink, win_out, diff_lambda, diff_subln_g,
           diff_out, w_out, ffn_w_gate, ffn_w_up, ffn_w_down, moe_router, moe_router_b,
           moe_w_gate, moe_w_up, moe_w_down, final_norm_g):
    B, n_lat, D = x.shape
    n_ctx = ctx.shape[1]
    n_layers = w_in.shape[0]
    P = dict(norm1_g=norm1_g, norm2_g=norm2_g, w_in=w_in, pool_w=pool_w, pool_scale=pool_scale,
             pool_out=pool_out, mla_kv_norm_g=mla_kv_norm_g, mla_w_kv_b=mla_w_kv_b, mla_out=mla_out,
             win_sink=win_sink, win_out=win_out, diff_lambda=diff_lambda, diff_subln_g=diff_subln_g,
             diff_out=diff_out, w_out=w_out, ffn_w_gate=ffn_w_gate, ffn_w_up=ffn_w_up,
             ffn_w_down=ffn_w_down, moe_router=moe_router, moe_router_b=moe_router_b,
             moe_w_gate=moe_w_gate, moe_w_up=moe_w_up, moe_w_down=moe_w_down)
    tabs = _rope_tables(n_lat, n_ctx)

    cond = jnp.concatenate([c, c_ctx[None]], axis=0)
    cond = cond * (1.0 / (1.0 + jnp.exp(-cond)))
    n_pad = (-cond.shape[0]) % 8
    cond = jnp.pad(cond, ((0, n_pad), (0, 0)))[None]

    state = {"h": jnp.concatenate([x, ctx], axis=1), "y": None, "gate": None}
    for l in range(n_layers):
        mod = _mm(cond, w_mod[l], rows=cond.shape[1], out_dtype=F32, name="mod")[0, :B + 1] + b_mod[l]
        mods = [m.reshape(B + 1, 1, D) for m in jnp.split(mod, 6, axis=-1)]
        state = _layer(state, l, n_layers, P, mods, tabs, n_lat=n_lat)
    return _resnorm(state["h"], final_norm_g, rows=n_lat, n_lat=n_lat, y=state["y"], y_w=state["y_w"],
                    gate=state["gate"], out_dtype=F32)
```

```python
import functools
import math

import jax
import jax.numpy as jnp
from jax import lax
from jax.experimental import pallas as pl
from jax.experimental.pallas import tpu as pltpu

F32 = jnp.float32
BF16 = jnp.bfloat16

GRID_W = 64
ROPE_BASE = 10000.0
NORM_EPS = 1e-6
NEG_INF = -1e30
POOL_GROUPS = 4
POOL_WINDOWS = (2, 4, 8, 16)
MLA_HEADS = 8
MLA_NOPE = 128
MLA_ROPE = 64
MLA_V = 128
WIN_HEADS = 8
WIN_KV_HEADS = 2
WIN_HEAD_DIM = 128
WINDOW = 128
DIFF_HEADS = 4
DIFF_HEAD_DIM = 128
N_BRANCHES = 4
N_EXPERTS = 8
TOP_K = 2

LOG2E = math.log2(math.e)
LANES = 128
MXU_DIM = 256
VMEM_LIMIT = 56 * 1024 * 1024


def _tile(n, cap, mult):
    best = None
    for t in range(mult, min(n, cap) + 1, mult):
        if n % t == 0:
            best = t
    assert best is not None, (n, cap, mult)
    return best


def _params(sem):
    return pltpu.CompilerParams(dimension_semantics=sem, vmem_limit_bytes=VMEM_LIMIT)


def _resnorm_kernel(*refs, has_y, n_slots, has_mod, emit_h):
    it = iter(refs)
    h_ref = next(it)
    y_refs = [next(it) for _ in range(max(n_slots, 1))] if has_y else []
    yw_ref = next(it) if n_slots else None
    gate_ref = next(it) if has_y else None
    g_ref = next(it)
    sh_ref = next(it) if has_mod else None
    sc_ref = next(it) if has_mod else None
    hout_ref = next(it) if emit_h else None
    u_ref = next(it)
    h = h_ref[...]
    if has_y:
        if n_slots:
            y = sum(yw_ref[:, k:k + 1] * y_refs[k][...] for k in range(n_slots))
        else:
            y = y_refs[0][...].astype(F32)
        h = h + gate_ref[...] * y
    if emit_h:
        hout_ref[...] = h
    ms = jnp.mean(h * h, axis=-1, keepdims=True)
    v = h * lax.rsqrt(ms + NORM_EPS) * g_ref[...]
    if has_mod:
        v = v * (1.0 + sc_ref[...]) + sh_ref[...]
    u_ref[...] = v.astype(u_ref.dtype)


def _resnorm(h, norm_g, *, rows, n_lat, y=None, y_w=None, gate=None, shift=None, scale=None, emit_h=False,
             out_dtype=BF16):
    B, _, D = h.shape
    ctx = rows - n_lat
    tm = _tile(math.gcd(n_lat, ctx) if ctx else n_lat, 512, 16)
    n_lat_tiles = n_lat // tm
    has_y, has_mod = y is not None, shift is not None
    n_slots = 0 if y_w is None else y_w.shape[-1]
    row_spec = pl.BlockSpec((None, tm, D), lambda b, i: (b, i, 0))
    vec_spec = pl.BlockSpec((None, 1, D), lambda b, i: (jnp.where(i < n_lat_tiles, b, B), 0, 0))
    args, specs = [h], [row_spec]
    if n_slots:
        per_b = rows // tm
        args += [y] * n_slots + [y_w, gate]
        specs += [pl.BlockSpec((tm, D), lambda b, i, k=k: ((k * B + b) * per_b + i, 0)) for k in range(n_slots)]
        specs += [pl.BlockSpec((None, tm, n_slots), lambda b, i: (b, i, 0)), vec_spec]
    elif has_y:
        args += [y, gate]
        specs += [row_spec, vec_spec]
    args.append(norm_g.reshape(1, D))
    specs.append(pl.BlockSpec((1, D), lambda b, i: (0, 0)))
    if has_mod:
        args += [shift, scale]
        specs += [vec_spec, vec_spec]
    out_shape, out_specs = [], []
    if emit_h:
        out_shape.append(jax.ShapeDtypeStruct((B, rows, D), F32))
        out_specs.append(row_spec)
    out_shape.append(jax.ShapeDtypeStruct((B, rows, D), out_dtype))
    out_specs.append(row_spec)
    res = pl.pallas_call(
        functools.partial(_resnorm_kernel, has_y=has_y, n_slots=n_slots, has_mod=has_mod, emit_h=emit_h),
        grid=(B, rows // tm), in_specs=specs, out_specs=out_specs, out_shape=out_shape,
        compiler_params=_params(("parallel", "parallel")), name="resnorm")(*args)
    return res if emit_h else res[0]


def _mm_kernel(*refs, has_g, act):
    if has_g:
        a_ref, w_ref, g_ref, o_ref = refs
        af = a_ref[...].astype(F32)
        af = af * lax.rsqrt(jnp.mean(af * af, axis=-1, keepdims=True) + NORM_EPS) * g_ref[...]
        a = af.astype(BF16)
    else:
        a_ref, w_ref, o_ref = refs
        a = a_ref[...].astype(BF16)
    acc = jnp.dot(a, w_ref[...].astype(BF16), preferred_element_type=F32)
    if act == "sigmoid":
        acc = 1.0 / (1.0 + jnp.exp(-acc))
    o_ref[...] = acc.astype(o_ref.dtype)


def _mm(a, w, *, rows, k_block=0, k_width=None, rms_g=None, act=None, out_dtype=BF16,
        tm_cap=1100, tn_cap=1024, name="mm"):
    B = a.shape[0]
    K, N = w.shape
    if k_width is None:
        assert a.shape[2] == K
    tm = _tile(rows, tm_cap, 8 if rows < 16 else 16)
    tn = _tile(N, tn_cap, MXU_DIM if N % MXU_DIM == 0 else LANES)
    args = [a, w]
    specs = [pl.BlockSpec((None, tm, K), lambda b, i, j: (b, i, k_block)),
             pl.BlockSpec((K, tn), lambda b, i, j: (0, j))]
    if rms_g is not None:
        args.append(rms_g.reshape(1, K).astype(F32))
        specs.append(pl.BlockSpec((1, K), lambda b, i, j: (0, 0)))
    return pl.pallas_call(
        functools.partial(_mm_kernel, has_g=rms_g is not None, act=act),
        grid=(B, rows // tm, N // tn), in_specs=specs,
        out_specs=pl.BlockSpec((None, tm, tn), lambda b, i, j: (b, i, j)),
        out_shape=jax.ShapeDtypeStruct((B, rows, N), out_dtype),
        compiler_params=_params(("parallel", "parallel", "arbitrary")), name=name)(*args)


def _mm_rope_kernel(a_ref, w_ref, cos_ref, sin_hi_ref, sin_lo_ref, o_ref, *, half):
    acc = jnp.dot(a_ref[...], w_ref[...], preferred_element_type=F32)
    cos, sin_hi, sin_lo = cos_ref[...], sin_hi_ref[...], sin_lo_ref[...]
    for hd in range(acc.shape[1] // LANES):
        x = acc[:, hd * LANES:(hd + 1) * LANES]
        y = x * cos + pltpu.roll(x, half, 1) * sin_hi + pltpu.roll(x, LANES - half, 1) * sin_lo
        o_ref[:, hd * LANES:(hd + 1) * LANES] = y.astype(o_ref.dtype)


def _mm_rope(a, w, tables, *, half, rows, tn_cap, name):
    B, _, K = a.shape
    N = w.shape[1]
    tm = _tile(rows, 1100, 16)
    tn = _tile(N, tn_cap, LANES)
    tab_spec = pl.BlockSpec((tm, LANES), lambda b, i, j: (i, 0))
    return pl.pallas_call(
        functools.partial(_mm_rope_kernel, half=half), grid=(B, rows // tm, N // tn),
        in_specs=[pl.BlockSpec((None, tm, K), lambda b, i, j: (b, i, 0)),
                  pl.BlockSpec((K, tn), lambda b, i, j: (0, j)), tab_spec, tab_spec, tab_spec],
        out_specs=pl.BlockSpec((None, tm, tn), lambda b, i, j: (b, i, j)),
        out_shape=jax.ShapeDtypeStruct((B, rows, N), BF16),
        compiler_params=_params(("parallel", "parallel", "arbitrary")), name=name)(a, w, *tables)


POOL_HALO = 16


def _pool_kernel(prev_ref, x_ref, next_ref, w_ref, o_ref, *, n_lat, n_rows):
    tm = x_ref.shape[0]
    i = pl.program_id(1)
    half_w = jnp.left_shift(1, pl.program_id(2))
    xh = jnp.concatenate([prev_ref[...], x_ref[...], next_ref[...]], axis=0)
    shape = (tm, tm + 2 * POOL_HALO)
    row0 = i * tm
    r = lax.broadcasted_iota(jnp.int32, shape, 0) + row0
    src = lax.broadcasted_iota(jnp.int32, shape, 1) + (row0 - POOL_HALO)
    latent = row0 < n_lat
    lo = jnp.maximum(r - half_w, jnp.where(latent, 0, n_lat))
    hi = jnp.minimum(r + half_w, jnp.where(latent, n_lat, n_rows))
    band = jnp.where((src >= lo) & (src < hi), 1.0 / (hi - lo).astype(F32), 0.0) - jnp.where(src == r, 1.0, 0.0)
    pooled = jnp.dot(band.astype(BF16), xh, preferred_element_type=F32)
    o_ref[...] = jnp.dot(pooled.astype(BF16), w_ref[...], preferred_element_type=F32).astype(o_ref.dtype)


def _pool_mix(z, w, *, rows, n_lat):
    B, T, _ = z.shape
    G, gd, _ = w.shape
    assert tuple(2 << g for g in range(G)) == POOL_WINDOWS and max(POOL_WINDOWS) // 2 <= POOL_HALO
    ctx = rows - n_lat
    tm = _tile(math.gcd(n_lat, ctx) if ctx else n_lat, 512, POOL_HALO)
    per = tm // POOL_HALO
    last = T // POOL_HALO - 1
    return pl.pallas_call(
        functools.partial(_pool_kernel, n_lat=n_lat, n_rows=rows), grid=(B, rows // tm, G),
        in_specs=[pl.BlockSpec((None, POOL_HALO, gd), lambda b, i, g: (b, jnp.maximum(i * per - 1, 0), g)),
                  pl.BlockSpec((None, tm, gd), lambda b, i, g: (b, i, g)),
                  pl.BlockSpec((None, POOL_HALO, gd), lambda b, i, g: (b, jnp.minimum((i + 1) * per, last), g)),
                  pl.BlockSpec((None, gd, gd), lambda b, i, g: (g, 0, 0))],
        out_specs=pl.BlockSpec((None, tm, gd), lambda b, i, g: (b, i, g)),
        out_shape=jax.ShapeDtypeStruct((B, rows, G * gd), BF16),
        compiler_params=_params(("parallel", "parallel", "arbitrary")), name="pool_mix")(z, z, z, w)


def _merge_kernel(a0, a1, a2, a3, w0, w1, w2, w3, g0, g1, g2, g3, o_ref):
    acc = None
    for a_ref, w_ref, g_ref in ((a0, w0, g0), (a1, w1, g1), (a2, w2, g2), (a3, w3, g3)):
        y = jnp.dot(a_ref[...], w_ref[...], preferred_element_type=F32) * g_ref[...].astype(F32)
        acc = y if acc is None else acc + y
    o_ref[...] = acc.astype(o_ref.dtype)


def _merge(branches, weights, gates, *, rows):
    B = branches[0].shape[0]
    N = weights[0].shape[1]
    tm = _tile(rows, 1100, 16)
    tn = _tile(N, 512, LANES)
    nj = N // tn
    a_specs = [pl.BlockSpec((None, tm, w.shape[0]), lambda b, i, j: (b, i, 0)) for w in weights]
    w_specs = [pl.BlockSpec((w.shape[0], tn), lambda b, i, j: (0, j)) for w in weights]
    g_specs = [pl.BlockSpec((None, tm, tn), lambda b, i, j, br=br: (b, i, br * nj + j))
               for br in range(N_BRANCHES)]
    return pl.pallas_call(
        _merge_kernel, grid=(B, rows // tm, nj),
        in_specs=a_specs + w_specs + g_specs,
        out_specs=pl.BlockSpec((None, tm, tn), lambda b, i, j: (b, i, j)),
        out_shape=jax.ShapeDtypeStruct((B, rows, N), BF16),
        compiler_params=_params(("parallel", "parallel", "arbitrary")),
        name="merge")(*branches, *weights, gates, gates, gates, gates)


def _ffn_kernel(*refs, has_rs, n_steps):
    if has_rs:
        a_ref, wg_ref, wu_ref, wd_ref, rs_ref, o_ref, acc_ref = refs
    else:
        a_ref, wg_ref, wu_ref, wd_ref, o_ref, acc_ref = refs
    k = pl.program_id(2)

    @pl.when(k == 0)
    def _():
        acc_ref[...] = jnp.zeros_like(acc_ref)

    a = a_ref[...]
    g = jnp.dot(a, wg_ref[...], preferred_element_type=F32)
    u = jnp.dot(a, wu_ref[...], preferred_element_type=F32)
    hid = g * (1.0 / (1.0 + jnp.exp(-g))) * u
    if has_rs:
        hid = hid * rs_ref[...]
    acc_ref[...] += jnp.dot(hid.astype(BF16), wd_ref[...], preferred_element_type=F32)

    @pl.when(k == n_steps - 1)
    def _():
        o_ref[...] = acc_ref[...].astype(o_ref.dtype)


def _ffn(a, wg, wu, wd, *, rows, row_scale=None):
    B, _, D = a.shape
    E, _, F = wg.shape
    tm = _tile(rows, 1100, 16)
    tf = _tile(F, 512, LANES)
    nf = F // tf
    n_steps = E * nf
    args = [a, wg, wu, wd]
    specs = [pl.BlockSpec((None, tm, D), lambda b, i, k: (b, i, 0)),
             pl.BlockSpec((None, D, tf), lambda b, i, k: (k // nf, 0, k % nf)),
             pl.BlockSpec((None, D, tf), lambda b, i, k: (k // nf, 0, k % nf)),
             pl.BlockSpec((None, tf, D), lambda b, i, k: (k // nf, k % nf, 0))]
    if row_scale is not None:
        args.append(row_scale)
        specs.append(pl.BlockSpec((None, None, tm, 1), lambda b, i, k: (b, k // nf, i, 0)))
    return pl.pallas_call(
        functools.partial(_ffn_kernel, has_rs=row_scale is not None, n_steps=n_steps),
        grid=(B, rows // tm, n_steps), in_specs=specs,
        out_specs=pl.BlockSpec((None, tm, D), lambda b, i, k: (b, i, 0)),
        out_shape=jax.ShapeDtypeStruct((B, rows, D), BF16),
        scratch_shapes=[pltpu.VMEM((tm, D), F32)],
        compiler_params=_params(("parallel", "parallel", "arbitrary")), name="ffn")(*args)


def _moe_kernel(te_ref, nu_ref, x_hbm, ord_hbm, wg_ref, wu_ref, wd_ref, y_hbm,
                gbuf, xb, acc, idx, sem_g, sem_s, sem_i, *, tm, nf, n_tok):
    del te_ref
    t = pl.program_id(0)
    f = pl.program_id(1)
    n_used = nu_ref[0]
    slot = t % 2

    def idx_copy(tile, s):
        return pltpu.make_async_copy(ord_hbm.at[pl.ds(tile * (2 * tm), 2 * tm)], idx.at[s], sem_i)

    def gather_copy(tok, r):
        return pltpu.make_async_copy(x_hbm.at[pl.ds(tok, 1)], gbuf.at[pl.ds(r, 1)], sem_g)

    def scatter_copy(s, r, dst):
        return pltpu.make_async_copy(acc.at[s, pl.ds(r, 1)], y_hbm.at[pl.ds(dst, 1)], sem_s.at[s])

    def for_rows(fn):
        def body(i, c):
            base = pl.multiple_of(i * SUBLANES, SUBLANES)
            for j in range(SUBLANES):
                fn(base + j)
            return c
        lax.fori_loop(0, tm // SUBLANES, body, 0)

    def issue_gather(s):
        for_rows(lambda r: gather_copy(idx[s, r], r).start())

    def wait_gather():
        pltpu.make_async_copy(x_hbm.at[pl.ds(0, tm)], gbuf, sem_g).wait()

    def issue_scatter(s):
        for_rows(lambda r: scatter_copy(s, r, idx[s, tm + r]).start())

    def wait_scatter(s):
        pltpu.make_async_copy(acc.at[s], y_hbm.at[pl.ds(0, tm)], sem_s.at[s]).wait()

    @pl.when(t < n_used)
    def _():
        @pl.when(f == 0)
        def _():
            @pl.when(t == 0)
            def _():
                first = idx_copy(0, 0)
                first.start()
                first.wait()
                issue_gather(0)

            wait_gather()
            xb[...] = gbuf[...].astype(BF16)

            @pl.when(t + 1 < n_used)
            def _():
                idx_copy(t + 1, 1 - slot).start()

        @pl.when((f == 1) & (t + 1 < n_used))
        def _():
            idx_copy(t + 1, 1 - slot).wait()
            issue_gather(1 - slot)

        a = xb[...]
        g = jnp.dot(a, wg_ref[...], preferred_element_type=F32)
        u = jnp.dot(a, wu_ref[...], preferred_element_type=F32)
        hid = (g * (1.0 / (1.0 + jnp.exp(-g))) * u).astype(BF16)
        part = jnp.dot(hid, wd_ref[...], preferred_element_type=F32)

        @pl.when(f == 0)
        def _():
            acc[slot] = part

        @pl.when(f > 0)
        def _():
            acc[slot] += part

        @pl.when(f == nf - 1)
        def _():
            issue_scatter(slot)

            @pl.when(t > 0)
            def _():
                wait_scatter(1 - slot)

            @pl.when(t == n_used - 1)
            def _():
                wait_scatter(slot)


MOE_TILE_ROWS = 1024
SUBLANES = 8


def _router_kernel(a_ref, w_ref, b_ref, tw_ref, ti_ref):
    logits = jnp.dot(a_ref[...].astype(BF16), w_ref[...].astype(BF16), preferred_element_type=F32) + b_ref[...]
    lane = lax.broadcasted_iota(jnp.int32, logits.shape, 1).astype(F32)
    v1 = jnp.max(logits, axis=-1, keepdims=True)
    i1 = jnp.min(jnp.where(logits == v1, lane, float(LANES)), axis=-1, keepdims=True)
    rest = jnp.where(lane == i1, NEG_INF, logits)
    v2 = jnp.max(rest, axis=-1, keepdims=True)
    i2 = jnp.min(jnp.where(rest == v2, lane, float(LANES)), axis=-1, keepdims=True)
    e = jnp.exp(v2 - v1)
    w1 = 1.0 / (1.0 + e)
    tw_ref[...] = jnp.where(lane == 0.0, w1, jnp.where(lane == 1.0, e * w1, 0.0))
    ti_ref[...] = jnp.where(lane == 0.0, i1, jnp.where(lane == 1.0, i2, 0.0)).astype(jnp.int32)


def _router(u, w, b, *, rows):
    B, _, D = u.shape
    E = w.shape[1]
    assert TOP_K == 2
    tm = _tile(rows, 1100, 16)
    wp = jnp.pad(w, ((0, 0), (0, LANES - E)))
    bp = jnp.pad(b.astype(F32), (0, LANES - E), constant_values=NEG_INF).reshape(1, LANES)
    row_spec = pl.BlockSpec((None, tm, LANES), lambda b_, i: (b_, i, 0))
    tw, ti = pl.pallas_call(
        _router_kernel, grid=(B, rows // tm),
        in_specs=[pl.BlockSpec((None, tm, D), lambda b_, i: (b_, i, 0)),
                  pl.BlockSpec((D, LANES), lambda b_, i: (0, 0)), pl.BlockSpec((1, LANES), lambda b_, i: (0, 0))],
        out_specs=[row_spec, row_spec],
        out_shape=[jax.ShapeDtypeStruct((B, rows, LANES), F32), jax.ShapeDtypeStruct((B, rows, LANES), jnp.int32)],
        compiler_params=_params(("parallel", "parallel")), name="router")(u, wp, bp)
    return tw[:, :, :TOP_K], ti[:, :, :TOP_K]


def _moe_route(top_i, tm):
    n_tok = top_i.shape[0]
    e_flat = top_i.reshape(-1).astype(jnp.int32)
    n_asg = e_flat.shape[0]
    cnt = jnp.sum(jax.nn.one_hot(e_flat, N_EXPERTS, dtype=jnp.int32), axis=0)
    pad = (-cnt) % tm
    big = 2 * N_EXPERTS
    j = jnp.arange(tm, dtype=jnp.int32)[None, :]
    e = jnp.arange(N_EXPERTS, dtype=jnp.int32)[:, None]
    pad_keys = jnp.where(j < pad[:, None], 2 * e + 1, big).reshape(-1)
    keys = jnp.concatenate([2 * e_flat, pad_keys])
    vals = jnp.concatenate([jnp.arange(n_asg, dtype=jnp.int32), jnp.full((N_EXPERTS * tm,), -1, jnp.int32)])
    keys, order = lax.sort_key_val(keys, vals)
    n_tiles = keys.shape[0] // tm
    tile_key = keys.reshape(n_tiles, tm)[:, 0]
    used = tile_key < big
    n_used = jnp.sum(used).astype(jnp.int32)
    tile_e = jnp.where(used, tile_key // 2, 0)
    tile_e = jnp.where(used, tile_e, tile_e[jnp.maximum(n_used - 1, 0)])
    src = jnp.maximum(order, 0) // TOP_K
    spare = TOP_K * n_tok + jnp.arange(order.shape[0], dtype=jnp.int32) % tm
    dst = jnp.where(order >= 0, (order % TOP_K) * n_tok + order // TOP_K, spare)
    rows = jnp.concatenate([src.reshape(n_tiles, tm), dst.reshape(n_tiles, tm)], axis=1).reshape(-1)
    return rows, tile_e.astype(jnp.int32), n_used.reshape(1)


def _moe(x, top_i, wg, wu, wd):
    n_tok, D = x.shape
    E, _, F = wg.shape
    tm = _tile(n_tok, MOE_TILE_ROWS, 16)
    tf = _tile(F, 512, LANES)
    nf = F // tf
    assert nf >= 2
    order, tile_e, n_used = _moe_route(top_i, tm)
    n_tiles = order.shape[0] // (2 * tm)

    def w_in_map(t, f, te, nu):
        return (te[t], 0, jnp.where(t < nu[0], f, nf - 1))

    def w_out_map(t, f, te, nu):
        return (te[t], jnp.where(t < nu[0], f, nf - 1), 0)

    grid_spec = pltpu.PrefetchScalarGridSpec(
        num_scalar_prefetch=2, grid=(n_tiles, nf),
        in_specs=[pl.BlockSpec(memory_space=pl.ANY), pl.BlockSpec(memory_space=pl.ANY),
                  pl.BlockSpec((None, D, tf), w_in_map), pl.BlockSpec((None, D, tf), w_in_map),
                  pl.BlockSpec((None, tf, D), w_out_map)],
        out_specs=pl.BlockSpec(memory_space=pl.ANY),
        scratch_shapes=[pltpu.VMEM((tm, D), F32), pltpu.VMEM((tm, D), BF16), pltpu.VMEM((2, tm, D), F32),
                        pltpu.SMEM((2, 2 * tm), jnp.int32), pltpu.SemaphoreType.DMA(()),
                        pltpu.SemaphoreType.DMA((2,)), pltpu.SemaphoreType.DMA(())])
    y = pl.pallas_call(
        functools.partial(_moe_kernel, tm=tm, nf=nf, n_tok=n_tok), grid_spec=grid_spec,
        out_shape=jax.ShapeDtypeStruct((TOP_K * n_tok + tm, D), F32),
        compiler_params=_params(("arbitrary", "arbitrary")), name="moe")(
            tile_e, n_used, x, order, wg, wu, wd)
    return y


ATTN_CHAIN_ROWS = 256


def _softmax_rows(s):
    p = jnp.exp2(s - jnp.max(s, axis=-1, keepdims=True))
    return p.astype(BF16), jnp.sum(p, axis=-1, keepdims=True)


def _chains(tq):
    r = min(tq, ATTN_CHAIN_ROWS)
    return [slice(a * r, (a + 1) * r) for a in range(tq // r)]


def _qk(q, k):
    return lax.dot_general(q, k, (((1,), (1,)), ((), ())), preferred_element_type=F32)


def _mla_kernel(qn_ref, qr_ref, kn_ref, kr_ref, v_ref, *rest):
    o_ref = rest[-1]
    k = jnp.concatenate([kn_ref[...], kr_ref[...]], axis=-1)
    v = v_ref[...]
    for rows in _chains(qn_ref.shape[0]):
        q = jnp.concatenate([qn_ref[rows, :], qr_ref[rows, :]], axis=-1)
        p, l = _softmax_rows(_qk(q, k))
        o_ref[rows, :] = (jnp.dot(p, v, preferred_element_type=F32) / l).astype(o_ref.dtype)


def _attn_calls(kernel, make_specs, args, out_width, vec_args, *, B, H, T, n_lat, need_ctx, tq_cap, name):
    ctx = T - n_lat
    out_rows = T if need_ctx else n_lat

    def call(tq, q_blk0, nq, k_rows, k_blk, prev):
        specs = make_specs(tq, q_blk0, k_rows, k_blk)
        specs += [pl.BlockSpec(v.shape, lambda b, h, i: (0, 0)) for v in vec_args]
        ins = list(args) + list(vec_args)
        aliases = {}
        if prev is not None:
            specs.append(pl.BlockSpec(memory_space=pl.ANY))
            aliases = {len(ins): 0}
            ins.append(prev)
        return pl.pallas_call(
            kernel, grid=(B, H, nq), in_specs=specs,
            out_specs=pl.BlockSpec((None, tq, out_width), lambda b, h, i: (b, q_blk0 + i, h)),
            out_shape=jax.ShapeDtypeStruct((B, out_rows, H * out_width), BF16),
            input_output_aliases=aliases,
            compiler_params=_params(("parallel", "parallel", "arbitrary")), name=name)(*ins)

    tq = _tile(n_lat, tq_cap, ATTN_CHAIN_ROWS)
    out = call(tq, 0, n_lat // tq, T, 0, None)
    if need_ctx:
        out = call(ctx, n_lat // ctx, 1, ctx, n_lat // ctx, out)
    return out


def _mla_attn(z, zr, kv, *, n_lat, need_ctx, cols):
    B, T, _ = z.shape
    H = MLA_HEADS
    qn0, qr0, kr0 = cols["q_nope"], cols["q_rope"], cols["k_rope"]

    def specs(tq, q_blk0, k_rows, k_blk):
        return [pl.BlockSpec((None, tq, LANES), lambda b, h, i: (b, q_blk0 + i, qn0 + h)),
                pl.BlockSpec((None, tq, LANES), lambda b, h, i: (b, q_blk0 + i, qr0 + h)),
                pl.BlockSpec((None, k_rows, LANES), lambda b, h, i: (b, k_blk, h)),
                pl.BlockSpec((None, k_rows, LANES), lambda b, h, i: (b, k_blk, kr0)),
                pl.BlockSpec((None, k_rows, LANES), lambda b, h, i: (b, k_blk, H + h))]

    return _attn_calls(_mla_kernel, specs, (z, zr, kv, zr, kv), MLA_V, (), B=B, H=H, T=T, n_lat=n_lat,
                       need_ctx=need_ctx, tq_cap=2048, name="mla_attn")


def _diff_kernel(q_ref, k_ref, v_ref, lam_ref, g_ref, *rest):
    o_ref = rest[-1]
    d = DIFF_HEAD_DIM
    k = k_ref[...]
    k1, k2 = k[:, :d], k[:, d:]
    v = v_ref[...]
    for rows in _chains(q_ref.shape[0]):
        q = q_ref[rows, :]
        p1, l1 = _softmax_rows(_qk(q[:, :d], k1))
        p2, l2 = _softmax_rows(_qk(q[:, d:], k2))
        o = (jnp.dot(p1, v, preferred_element_type=F32) / l1
             - lam_ref[...] * (jnp.dot(p2, v, preferred_element_type=F32) / l2))
        o = o * lax.rsqrt(jnp.mean(o * o, axis=-1, keepdims=True) + NORM_EPS) * g_ref[...]
        o_ref[rows, :] = o.astype(o_ref.dtype)


def _diff_attn(z, zr, lam_vec, g_vec, *, n_lat, need_ctx, cols):
    B, T, _ = z.shape
    dd = 2 * DIFF_HEAD_DIM
    q0, k0, v0 = cols["q_diff"], cols["k_diff"], cols["v_diff"]

    def specs(tq, q_blk0, k_rows, k_blk):
        return [pl.BlockSpec((None, tq, dd), lambda b, h, i: (b, q_blk0 + i, q0 + h)),
                pl.BlockSpec((None, k_rows, dd), lambda b, h, i: (b, k_blk, k0 + h)),
                pl.BlockSpec((None, k_rows, dd), lambda b, h, i: (b, k_blk, v0 + h))]

    return _attn_calls(_diff_kernel, specs, (zr, zr, z), dd, (lam_vec, g_vec), B=B, H=DIFF_HEADS, T=T,
                       n_lat=n_lat, need_ctx=need_ctx, tq_cap=1024, name="diff_attn")


def _win_kernel(q_ref, kp_ref, kc_ref, kn_ref, kx_ref, vp_ref, vc_ref, vn_ref, vx_ref, sink_ref,
                o_ref, *, n_lat):
    d = WIN_HEAD_DIM
    tq = q_ref.shape[0]
    i = pl.program_id(2)
    k = jnp.concatenate([kp_ref[...], kc_ref[...], kn_ref[...], kx_ref[...]], axis=0)
    v = jnp.concatenate([vp_ref[...], vc_ref[...], vn_ref[...], vx_ref[...]], axis=0)
    n_loc = tq + 2 * WINDOW
    nk = k.shape[0]
    r = lax.broadcasted_iota(jnp.int32, (tq, nk), 0)
    c = lax.broadcasted_iota(jnp.int32, (tq, nk), 1)
    kpos = i * tq - WINDOW + c
    rel = c - WINDOW - r
    valid = (c >= n_loc) | ((jnp.abs(rel) <= WINDOW) & (kpos >= 0) & (kpos < n_lat))
    outs = []
    for g in range(WIN_HEADS // WIN_KV_HEADS):
        q = q_ref[:, g * d:(g + 1) * d]
        s = lax.dot_general(q, k, (((1,), (1,)), ((), ())), preferred_element_type=F32)
        s = jnp.where(valid, s, NEG_INF)
        sink = sink_ref[:, g * d:g * d + 1]
        m = jnp.maximum(jnp.max(s, axis=-1, keepdims=True), sink)
        p = jnp.exp2(s - m)
        l = jnp.sum(p, axis=-1, keepdims=True) + jnp.exp2(sink - m)
        o = jnp.dot(p.astype(BF16), v, preferred_element_type=F32)
        outs.append(o / l)
    o_ref[...] = jnp.concatenate(outs, axis=-1).astype(o_ref.dtype)


def _win_attn(z, zr, sink_vec, *, n_lat, out_rows, cols):
    B, T, _ = z.shape
    d = WIN_HEAD_DIM
    grp = WIN_HEADS // WIN_KV_HEADS
    ctx = T - n_lat
    tq = _tile(n_lat, 512, WINDOW)
    per = tq // WINDOW
    q0, k0, v0 = cols["q_win"], cols["k_win"], cols["v_win"]

    def halo(col0):
        return [pl.BlockSpec((None, WINDOW, d), lambda b, h, i: (b, jnp.maximum(i * per - 1, 0), col0 + h)),
                pl.BlockSpec((None, tq, d), lambda b, h, i: (b, i, col0 + h)),
                pl.BlockSpec((None, WINDOW, d), lambda b, h, i: (b, (i + 1) * per, col0 + h)),
                pl.BlockSpec((None, ctx, d), lambda b, h, i: (b, n_lat // ctx, col0 + h))]

    return pl.pallas_call(
        functools.partial(_win_kernel, n_lat=n_lat), grid=(B, WIN_KV_HEADS, n_lat // tq),
        in_specs=[pl.BlockSpec((None, tq, grp * d), lambda b, h, i: (b, i, q0 + h))]
        + halo(k0) + halo(v0) + [pl.BlockSpec((1, grp * d), lambda b, h, i: (0, h))],
        out_specs=pl.BlockSpec((None, tq, grp * d), lambda b, h, i: (b, i, h)),
        out_shape=jax.ShapeDtypeStruct((B, out_rows, WIN_HEADS * d), BF16),
        compiler_params=_params(("parallel", "parallel", "arbitrary")),
        name="win_attn")(zr, zr, zr, zr, zr, z, z, z, z, sink_vec)


def _sink_kernel(q_ref, k_ref, v_ref, sink_ref, prev_ref, o_ref):
    del prev_ref
    d = WIN_HEAD_DIM
    k = k_ref[...]
    v = v_ref[...]
    outs = []
    for g in range(WIN_HEADS // WIN_KV_HEADS):
        q = q_ref[:, g * d:(g + 1) * d]
        s = lax.dot_general(q, k, (((1,), (1,)), ((), ())), preferred_element_type=F32)
        sink = sink_ref[:, g * d:g * d + 1]
        m = jnp.maximum(jnp.max(s, axis=-1, keepdims=True), sink)
        p = jnp.exp2(s - m)
        l = jnp.sum(p, axis=-1, keepdims=True) + jnp.exp2(sink - m)
        outs.append(jnp.dot(p.astype(BF16), v, preferred_element_type=F32) / l)
    o_ref[...] = jnp.concatenate(outs, axis=-1).astype(o_ref.dtype)


def _sink_attn(z, zr, sink_vec, prev, *, n_lat, cols):
    B, T, _ = z.shape
    d = WIN_HEAD_DIM
    grp = WIN_HEADS // WIN_KV_HEADS
    ctx = T - n_lat
    blk = n_lat // ctx
    q0, k0, v0 = cols["q_win"], cols["k_win"], cols["v_win"]
    return pl.pallas_call(
        _sink_kernel, grid=(B, WIN_KV_HEADS),
        in_specs=[pl.BlockSpec((None, ctx, grp * d), lambda b, h: (b, blk, q0 + h)),
                  pl.BlockSpec((None, ctx, d), lambda b, h: (b, blk, k0 + h)),
                  pl.BlockSpec((None, ctx, d), lambda b, h: (b, blk, v0 + h)),
                  pl.BlockSpec((1, grp * d), lambda b, h: (0, h)),
                  pl.BlockSpec(memory_space=pl.ANY)],
        out_specs=pl.BlockSpec((None, ctx, grp * d), lambda b, h: (b, blk, h)),
        out_shape=jax.ShapeDtypeStruct(prev.shape, prev.dtype),
        input_output_aliases={4: 0},
        compiler_params=_params(("parallel", "parallel")), name="sink_attn")(zr, zr, z, sink_vec, prev)


def _in_proj_layout(D):
    pool_w = POOL_GROUPS * (D // 8)
    widths = [("pool", pool_w), ("q_mla", MLA_HEADS * (MLA_NOPE + MLA_ROPE)), ("ckv", D // 4),
              ("k_rope", MLA_ROPE), ("q_win", WIN_HEADS * WIN_HEAD_DIM),
              ("k_win", WIN_KV_HEADS * WIN_HEAD_DIM), ("v_win", WIN_KV_HEADS * WIN_HEAD_DIM),
              ("q_diff", DIFF_HEADS * 2 * DIFF_HEAD_DIM), ("k_diff", DIFF_HEADS * 2 * DIFF_HEAD_DIM),
              ("v_diff", DIFF_HEADS * 2 * DIFF_HEAD_DIM), ("gates", N_BRANCHES * D)]
    out, off = {}, 0
    for name, w in widths:
        out[name] = (off, w)
        off += w
    return out


def _prep_in_proj(w_in, D):
    lay = _in_proj_layout(D)

    def cols(name):
        o, w = lay[name]
        return w_in[:, o:o + w]

    q_mla = cols("q_mla").reshape(D, MLA_HEADS, MLA_NOPE + MLA_ROPE) * (LOG2E * (MLA_NOPE + MLA_ROPE) ** -0.5)
    q_nope = q_mla[:, :, :MLA_NOPE].reshape(D, MLA_HEADS * MLA_NOPE)
    q_rope = jnp.pad(q_mla[:, :, MLA_NOPE:], ((0, 0), (0, 0), (0, LANES - MLA_ROPE))).reshape(D, MLA_HEADS * LANES)
    k_rope = jnp.pad(cols("k_rope"), ((0, 0), (0, LANES - MLA_ROPE)))
    plain = [cols("pool"), q_nope, cols("ckv"), cols("v_win"), cols("v_diff")]
    rot = [cols("q_win") * (LOG2E * WIN_HEAD_DIM ** -0.5), cols("q_diff") * (LOG2E * DIFF_HEAD_DIM ** -0.5),
           cols("k_win"), cols("k_diff")]
    rot_mla = [q_rope, k_rope]
    n_mla = sum(x.shape[1] for x in rot_mla)
    rot_mla.append(jnp.zeros((D, (-n_mla) % MXU_DIM), w_in.dtype))
    cat = lambda parts: jnp.concatenate(parts, axis=1).astype(BF16)
    return cat(plain), cat(rot), cat(rot_mla), cols("gates").astype(BF16)


def _rope_tables(n_lat, ctx):
    t = jnp.arange(n_lat)
    rows, cols_ = (t // GRID_W).astype(F32), (t % GRID_W).astype(F32)

    def table(half, pad):
        inv = ROPE_BASE ** (-jnp.arange(half, dtype=F32) / half)
        ar, ac = rows[:, None] * inv[None, :], cols_[:, None] * inv[None, :]
        zero = jnp.zeros_like(ar)
        cos = jnp.concatenate([jnp.cos(ar)] * 2 + [jnp.cos(ac)] * 2, axis=1)
        sin_hi = jnp.concatenate([zero, jnp.sin(ar), zero, jnp.sin(ac)], axis=1)
        sin_lo = jnp.concatenate([-jnp.sin(ar), zero, -jnp.sin(ac), zero], axis=1)
        cos = jnp.pad(cos, ((0, ctx), (0, pad)), constant_values=1.0)
        return cos, jnp.pad(sin_hi, ((0, ctx), (0, pad))), jnp.pad(sin_lo, ((0, ctx), (0, pad)))

    return table(WIN_HEAD_DIM // 4, 0), table(MLA_ROPE // 4, LANES - MLA_ROPE)


def _layer(h, l, n_layers, P, mods, tabs, *, n_lat):
    B, T, D = h["h"].shape
    need_ctx = l < n_layers - 1
    rows = T if need_ctx else n_lat
    lam_init = 0.8 - 0.6 * math.exp(-0.3 * l)
    sh1, sc1, gt1, sh2, sc2, gt2 = mods

    if h["y"] is None:
        u = _resnorm(h["h"], P["norm1_g"][l], rows=T, n_lat=n_lat, shift=sh1, scale=sc1)
        hres = h["h"]
    else:
        hres, u = _resnorm(h["h"], P["norm1_g"][l], rows=T, n_lat=n_lat, y=h["y"], y_w=h["y_w"], gate=h["gate"],
                           shift=sh1, scale=sc1, emit_h=True)

    w_plain, w_rot, w_rot_mla, w_gate = _prep_in_proj(P["w_in"][l], D)
    z = _mm(u, w_plain, rows=T, tn_cap=1536, name="in_proj")
    zr = _mm_rope(u, w_rot, tabs[0], half=WIN_HEAD_DIM // 4, rows=T, tn_cap=1792, name="in_proj_rot")
    zm = _mm_rope(u, w_rot_mla, tabs[1], half=MLA_ROPE // 4, rows=T, tn_cap=1792, name="in_proj_rot_mla")
    gates = _mm(u, w_gate, rows=rows, act="sigmoid", name="gate_proj")

    gd = D // 8
    pool_w = POOL_GROUPS * gd
    kvr = D // 4
    o_qn, o_ckv = pool_w, pool_w + MLA_HEADS * MLA_NOPE
    o_vw = o_ckv + kvr
    o_vd = o_vw + WIN_KV_HEADS * WIN_HEAD_DIM
    r_qd = WIN_HEADS * WIN_HEAD_DIM
    r_kw = r_qd + DIFF_HEADS * 2 * DIFF_HEAD_DIM
    r_kd = r_kw + WIN_KV_HEADS * WIN_HEAD_DIM
    dd = 2 * DIFF_HEAD_DIM
    for off, blk in ((o_qn, LANES), (o_ckv, kvr), (o_vw, LANES), (o_vd, dd), (r_qd, dd), (r_kw, LANES), (r_kd, dd)):
        assert off % blk == 0, (off, blk)
    cols = {"q_nope": o_qn // LANES, "q_rope": 0, "k_rope": MLA_HEADS,
            "q_diff": r_qd // dd, "k_diff": r_kd // dd, "v_diff": o_vd // dd,
            "q_win": 0, "k_win": r_kw // LANES, "v_win": o_vw // LANES}

    wkv = P["mla_w_kv_b"][l].reshape(kvr, MLA_HEADS, MLA_NOPE + MLA_V)
    wkv = jnp.concatenate([wkv[:, :, :MLA_NOPE].reshape(kvr, -1), wkv[:, :, MLA_NOPE:].reshape(kvr, -1)],
                          axis=1).astype(BF16)
    kv = _mm(z, wkv, rows=T, k_block=o_ckv // kvr, k_width=kvr, rms_g=P["mla_kv_norm_g"][l], name="kv_proj")

    mixed = _pool_mix(z, P["pool_w"][l].astype(BF16), rows=rows, n_lat=n_lat)

    mla_o = _mla_attn(z, zm, kv, n_lat=n_lat, need_ctx=need_ctx, cols=cols)
    lp = P["diff_lambda"][l].astype(F32)
    lam = jnp.exp(jnp.sum(lp[0] * lp[1])) - jnp.exp(jnp.sum(lp[2] * lp[3])) + lam_init
    lam_vec = jnp.full((1, dd), lam, F32)
    g_vec = (P["diff_subln_g"][l].astype(F32) * (1.0 - lam_init)).reshape(1, dd)
    diff_o = _diff_attn(z, zr, lam_vec, g_vec, n_lat=n_lat, need_ctx=need_ctx, cols=cols)
    sink_vec = jnp.repeat(P["win_sink"][l].astype(F32) * LOG2E, WIN_HEAD_DIM).reshape(1, WIN_HEADS * WIN_HEAD_DIM)
    win_o = _win_attn(z, zr, sink_vec, n_lat=n_lat, out_rows=rows, cols=cols)
    if need_ctx:
        win_o = _sink_attn(z, zr, sink_vec, win_o, n_lat=n_lat, cols=cols)

    w_pool_out = (P["pool_scale"][l][:, None] * P["pool_out"][l]).astype(BF16)
    merged = _merge([mixed, mla_o, win_o, diff_o],
                    [w_pool_out, P["mla_out"][l].astype(BF16), P["win_out"][l].astype(BF16),
                     P["diff_out"][l].astype(BF16)], gates, rows=rows)
    y1 = _mm(merged, P["w_out"][l].astype(BF16), rows=rows, name="out_proj")

    dense = l % 2 == 0
    h2, u2 = _resnorm(hres, P["norm2_g"][l], rows=rows, n_lat=n_lat, y=y1, gate=gt1, shift=sh2, scale=sc2,
                      emit_h=True, out_dtype=BF16 if dense else F32)
    j = l // 2
    if dense:
        y2 = _ffn(u2, P["ffn_w_gate"][j][None].astype(BF16), P["ffn_w_up"][j][None].astype(BF16),
                  P["ffn_w_down"][j][None].astype(BF16), rows=rows)
        return {"h": h2, "y": y2, "y_w": None, "gate": gt2}
    top_w, top_i = _router(u2, P["moe_router"][j], P["moe_router_b"][j], rows=rows)
    y2 = _moe(u2.reshape(B * rows, D), top_i.reshape(B * rows, TOP_K), P["moe_w_gate"][j].astype(BF16),
              P["moe_w_up"][j].astype(BF16), P["moe_w_down"][j].astype(BF16))
    return {"h": h2, "y": y2, "y_w": top_w, "gate": gt2}


def kernel(x, c, ctx, c_ctx, w_mod, b_mod, norm1_g, norm2_g, w_in, pool_w, pool_scale, pool_out,
           mla_kv_norm_g, mla_w_kv_b, mla_out, win_sink, win_out, diff_lambda, diff_subln_g,
           diff_out, w_out, ffn_w_gate, ffn_w_up, ffn_w_down, moe_router, moe_router_b,
           moe_w_gate, moe_w_up, moe_w_down, final_norm_g):
    B, n_lat, D = x.shape
    n_ctx = ctx.shape[1]
    n_layers = w_in.shape[0]
    P = dict(norm1_g=norm1_g, norm2_g=norm2_g, w_in=w_in, pool_w=pool_w, pool_scale=pool_scale,
             pool_out=pool_out, mla_kv_norm_g=mla_kv_norm_g, mla_w_kv_b=mla_w_kv_b, mla_out=mla_out,
             win_sink=win_sink, win_out=win_out, diff_lambda=diff_lambda, diff_subln_g=diff_subln_g,
             diff_out=diff_out, w_out=w_out, ffn_w_gate=ffn_w_gate, ffn_w_up=ffn_w_up,
             ffn_w_down=ffn_w_down, moe_router=moe_router, moe_router_b=moe_router_b,
             moe_w_gate=moe_w_gate, moe_w_up=moe_w_up, moe_w_down=moe_w_down)
    tabs = _rope_tables(n_lat, n_ctx)

    cond = jnp.concatenate([c, c_ctx[None]], axis=0)
    cond = cond * (1.0 / (1.0 + jnp.exp(-cond)))
    n_pad = (-cond.shape[0]) % 8
    cond = jnp.pad(cond, ((0, n_pad), (0, 0)))[None]

    state = {"h": jnp.concatenate([x, ctx], axis=1), "y": None, "gate": None}
    for l in range(n_layers):
        mod = _mm(cond, w_mod[l], rows=cond.shape[1], out_dtype=F32, name="mod")[0, :B + 1] + b_mod[l]
        mods = [m.reshape(B + 1, 1, D) for m in jnp.split(mod, 6, axis=-1)]
        state = _layer(state, l, n_layers, P, mods, tabs, n_lat=n_lat)
    return _resnorm(state["h"], final_norm_g, rows=n_lat, n_lat=n_lat, y=state["y"], y_w=state["y_w"],
                    gate=state["gate"], out_dtype=F32)
```
